```python
import math
import jax, jax.numpy as jnp
from jax import lax
import numpy as np

D_MODEL = 1024
BATCH = 16
SEQ = 4096
DEPTH = 1
DEC_BATCH = 8
DEC_SEQ = 32
PAST_LEN = 4096

CHUNK = 64
N_HEADS = 8
HEAD_DIM = 64
ATTN_WIDTH = N_HEADS * HEAD_DIM
CONV_WIDTH = D_MODEL // 2
CONV_K = 31
D_FF = 2816
FFN_CONV_K = 3
Q_BLOCK = 128
N_BRANCH = 2
N_MOD = 6
DN_ALPHA = (2 * DEPTH) ** 0.25
DN_BETA = (8 * DEPTH) ** -0.25
LN_EPS = 1e-5
FORGET_BIAS_INIT = 3.0
SPLIT_Q = ATTN_WIDTH
SPLIT_K = 2 * ATTN_WIDTH
SPLIT_V = 3 * ATTN_WIDTH
SPLIT_F = SPLIT_V + N_HEADS
SPLIT_GLU = SPLIT_F + 2 * CONV_WIDTH
IN_COLS = SPLIT_GLU + N_BRANCH * D_MODEL

kernel_name = "fox_conformer_convffn_stream_step"


def layer_norm(x, g, b):
    xf = x.astype(jnp.float32)
    mu = jnp.mean(xf, axis=-1, keepdims=True)
    var = jnp.mean(jnp.square(xf - mu), axis=-1, keepdims=True)
    return ((xf - mu) * lax.rsqrt(var + LN_EPS) * g.astype(jnp.float32) + b.astype(jnp.float32)).astype(x.dtype)


def causal_dwconv(x, hist, w, b):
    K, C = w.shape
    xp = jnp.concatenate([hist.astype(x.dtype), x], axis=1)
    y = lax.conv_general_dilated(
        xp, w[:, None, :].astype(x.dtype), window_strides=(1,), padding="VALID",
        dimension_numbers=("NWC", "WIO", "NWC"), feature_group_count=C)
    return y + b.astype(x.dtype), xp[:, -(K - 1):]


def fox_attention(q, k, v, F_q, F_k, q_offset):
    B, T, H, dh = q.shape
    L = k.shape[1]
    qb = min(Q_BLOCK, T)
    nb = T // qb
    k_pos = jnp.arange(L)
    Fk_t = jnp.transpose(F_k, (0, 2, 1))[:, :, None, :]
    scale = HEAD_DIM ** -0.5

    def block(i):
        start = i * qb
        qi = lax.dynamic_slice_in_dim(q, start, qb, axis=1)
        fi = lax.dynamic_slice_in_dim(F_q, start, qb, axis=1)
        q_pos = q_offset + start + jnp.arange(qb)
        s = jnp.einsum('bqhd,bkhd->bhqk', qi, k, preferred_element_type=jnp.float32) * scale
        s = s + jnp.transpose(fi, (0, 2, 1))[..., None] - Fk_t
        mask = k_pos[None, :] <= q_pos[:, None]
        s = jnp.where(mask, s, -jnp.inf)
        p = jax.nn.softmax(s, axis=-1)
        return jnp.einsum('bhqk,bkhd->bqhd', p.astype(v.dtype), v)

    out = lax.map(block, jnp.arange(nb))
    return jnp.moveaxis(out, 0, 1).reshape(B, T, H, dh)


def trunk_layer(x, c, p, past_k, past_v, past_logf, conv_hist, ffn_hist):
    B, T, _ = x.shape
    P = 0 if past_k is None else past_k.shape[1]
    mod = (c @ p['w_ada'] + p['b_ada'])[:, None, :]
    sh1, sc1, g1, sh2, sc2, g2 = jnp.split(mod, N_MOD, axis=-1)

    u = x * (1 + sc1) + sh1
    z = u @ p['w_in']
    q, k, v, f_logit, glu_in, gate_logits = jnp.split(
        z, [SPLIT_Q, SPLIT_K, SPLIT_V, SPLIT_F, SPLIT_GLU], axis=-1)
    q = q.reshape(B, T, N_HEADS, HEAD_DIM)
    k = k.reshape(B, T, N_HEADS, HEAD_DIM)
    v = v.reshape(B, T, N_HEADS, HEAD_DIM)
    logf = jax.nn.log_sigmoid((f_logit + p['b_f']).astype(jnp.float32))
    if past_k is None:
        k_all, v_all, logf_all = k, v, logf
    else:
        k_all = jnp.concatenate([past_k.astype(k.dtype), k], axis=1)
        v_all = jnp.concatenate([past_v.astype(v.dtype), v], axis=1)
        logf_all = jnp.concatenate([past_logf.astype(jnp.float32), logf], axis=1)
    F_all = jnp.cumsum(logf_all, axis=1)
    attn = fox_attention(q, k_all, v_all, F_all[:, P:], F_all, P)
    y_a = attn.reshape(B, T, ATTN_WIDTH) @ p['w_attn_proj']

    glu_a, glu_b = jnp.split(glu_in, 2, axis=-1)
    glu = glu_a * jax.nn.sigmoid(glu_b)
    hc, new_conv = causal_dwconv(glu, conv_hist, p['conv_w'], p['conv_b'])
    hc = jax.nn.silu(layer_norm(hc, p['conv_ln_g'], p['conv_ln_b']))
    y_b = hc @ p['w_conv_proj']

    gate_a, gate_b = jnp.split(gate_logits, N_BRANCH, axis=-1)
    merged = jax.nn.sigmoid(gate_a) * y_a + jax.nn.sigmoid(gate_b) * y_b
    x1 = layer_norm(DN_ALPHA * x + g1 * (merged @ p['w_out']), p['ln1_g'], p['ln1_b'])

    u2 = x1 * (1 + sc2) + sh2
    a2, v2 = jnp.split(u2 @ p['w_up'], 2, axis=-1)
    a2c, new_ffn = causal_dwconv(a2, ffn_hist, p['ffn_conv_w'], p['ffn_conv_b'])
    h = jax.nn.silu(a2c) * v2
    y = layer_norm(DN_ALPHA * x1 + g2 * (h @ p['w_down']), p['ln2_g'], p['ln2_b'])
    return y, k, v, logf, new_conv, new_ffn


def setup_inputs(seed: int = 0) -> dict:
    key = jax.random.key(seed)
    ks = iter(jax.random.split(key, 40))
    f32 = jnp.float32

    def nrm(shape, scale=1.0):
        return jax.random.normal(next(ks), shape, f32) * scale

    L = DEPTH
    d = {}
    d['x_prompt'] = nrm((BATCH, SEQ, D_MODEL))
    d['x_sample'] = nrm((DEC_BATCH, DEC_SEQ, D_MODEL))
    d['c_prompt'] = nrm((BATCH, D_MODEL))
    d['c_sample'] = nrm((DEC_BATCH, D_MODEL))
    d['cache_k'] = nrm((L, DEC_BATCH, PAST_LEN, N_HEADS, HEAD_DIM))
    d['cache_v'] = nrm((L, DEC_BATCH, PAST_LEN, N_HEADS, HEAD_DIM))
    d['cache_logf'] = jax.nn.log_sigmoid(FORGET_BIAS_INIT + nrm((L, DEC_BATCH, PAST_LEN, N_HEADS)))
    d['state_conv'] = nrm((L, DEC_BATCH, CONV_K - 1, CONV_WIDTH), 0.5)
    d['state_ffn_conv'] = nrm((L, DEC_BATCH, FFN_CONV_K - 1, D_FF))
    d['w_ada'] = nrm((L, D_MODEL, N_MOD * D_MODEL), 0.5 * D_MODEL ** -0.5)
    d['b_ada'] = nrm((L, N_MOD * D_MODEL), 0.02)
    d['w_in'] = nrm((L, D_MODEL, IN_COLS), D_MODEL ** -0.5)
    d['b_f'] = FORGET_BIAS_INIT + nrm((L, N_HEADS), 0.5)
    d['conv_w'] = nrm((L, CONV_K, CONV_WIDTH), CONV_K ** -0.5)
    d['conv_b'] = nrm((L, CONV_WIDTH), 0.02)
    d['conv_ln_g'] = 1.0 + nrm((L, CONV_WIDTH), 0.05)
    d['conv_ln_b'] = nrm((L, CONV_WIDTH), 0.02)
    d['w_attn_proj'] = nrm((L, ATTN_WIDTH, D_MODEL), DN_BETA * ATTN_WIDTH ** -0.5)
    d['w_conv_proj'] = nrm((L, CONV_WIDTH, D_MODEL), DN_BETA * CONV_WIDTH ** -0.5)
    d['w_out'] = nrm((L, D_MODEL, D_MODEL), DN_BETA * D_MODEL ** -0.5)
    d['ln1_g'] = 1.0 + nrm((L, D_MODEL), 0.05)
    d['ln1_b'] = nrm((L, D_MODEL), 0.02)
    d['w_up'] = nrm((L, D_MODEL, 2 * D_FF), D_MODEL ** -0.5)
    d['ffn_conv_w'] = nrm((L, FFN_CONV_K, D_FF), FFN_CONV_K ** -0.5)
    d['ffn_conv_b'] = nrm((L, D_FF), 0.02)
    d['w_down'] = nrm((L, D_FF, D_MODEL), DN_BETA * D_FF ** -0.5)
    d['ln2_g'] = 1.0 + nrm((L, D_MODEL), 0.05)
    d['ln2_b'] = nrm((L, D_MODEL), 0.02)
    return d


def reference(x_prompt, x_sample, c_prompt, c_sample, cache_k, cache_v, cache_logf,
              state_conv, state_ffn_conv, w_ada, b_ada, w_in, b_f, conv_w, conv_b,
              conv_ln_g, conv_ln_b, w_attn_proj, w_conv_proj, w_out, ln1_g, ln1_b,
              w_up, ffn_conv_w, ffn_conv_b, w_down, ln2_g, ln2_b):
    y_p, y_s = x_prompt, x_sample
    Bp = x_prompt.shape[0]
    kp_l, vp_l, fp_l, cp_l, ffp_l = [], [], [], [], []
    ks_l, vs_l, fs_l, cs_l, ffs_l = [], [], [], [], []
    for l in range(DEPTH):
        p = dict(w_ada=w_ada[l], b_ada=b_ada[l], w_in=w_in[l], b_f=b_f[l],
                 conv_w=conv_w[l], conv_b=conv_b[l], conv_ln_g=conv_ln_g[l], conv_ln_b=conv_ln_b[l],
                 w_attn_proj=w_attn_proj[l], w_conv_proj=w_conv_proj[l], w_out=w_out[l],
                 ln1_g=ln1_g[l], ln1_b=ln1_b[l], w_up=w_up[l], ffn_conv_w=ffn_conv_w[l],
                 ffn_conv_b=ffn_conv_b[l], w_down=w_down[l], ln2_g=ln2_g[l], ln2_b=ln2_b[l])
        zc = jnp.zeros((Bp, CONV_K - 1, CONV_WIDTH), x_prompt.dtype)
        zf = jnp.zeros((Bp, FFN_CONV_K - 1, D_FF), x_prompt.dtype)
        y_p, kp, vp, fp, cp, ffp = trunk_layer(y_p, c_prompt, p, None, None, None, zc, zf)
        y_s, ks_, vs_, fs_, cs_, ffs_ = trunk_layer(
            y_s, c_sample, p, cache_k[l], cache_v[l], cache_logf[l], state_conv[l], state_ffn_conv[l])
        kp_l.append(kp); vp_l.append(vp); fp_l.append(fp); cp_l.append(cp); ffp_l.append(ffp)
        ks_l.append(ks_); vs_l.append(vs_); fs_l.append(fs_); cs_l.append(cs_); ffs_l.append(ffs_)
    k_prompt = jnp.stack(kp_l)
    v_prompt = jnp.stack(vp_l)
    logf_prompt = jnp.stack(fp_l)
    conv_prompt = jnp.stack(cp_l)
    ffn_conv_prompt = jnp.stack(ffp_l)
    k_sample = jnp.stack(ks_l)
    v_sample = jnp.stack(vs_l)
    logf_sample = jnp.stack(fs_l)
    conv_sample = jnp.stack(cs_l)
    ffn_conv_sample = jnp.stack(ffs_l)
    return (y_p, y_s, k_prompt, v_prompt, logf_prompt, conv_prompt, ffn_conv_prompt,
            k_sample, v_sample, logf_sample, conv_sample, ffn_conv_sample)
```

```python
import functools

import jax
import jax.numpy as jnp
from jax import lax
from jax.experimental import pallas as pl
from jax.experimental.pallas import tpu as pltpu

F32 = jnp.float32
BF16 = jnp.bfloat16

LN_EPS = 1e-5
N_MOD = 6
HEAD_DIM = 64
LANES = 128
SUBLANES = 8
HEADS_PER_BLOCK = LANES // HEAD_DIM
CONV_HALO = 32
FFN_HALO = 8
VMEM_LIMIT = 56 * 1024 * 1024
NEG_INIT = -1e30


def _sigmoid(x):
    return 0.5 * (jnp.tanh(0.5 * x) + 1.0)


def _log_sigmoid(x):
    return jnp.minimum(x, 0.0) - jnp.log1p(jnp.exp(-jnp.abs(x)))


def _layer_norm(x, g, b):
    mu = jnp.mean(x, axis=-1, keepdims=True)
    xc = x - mu
    var = jnp.mean(xc * xc, axis=-1, keepdims=True)
    return xc * lax.rsqrt(var + LN_EPS) * g + b


def _const_spec(shape):
    zeros = (0,) * len(shape)
    return pl.BlockSpec(shape, lambda *_: zeros)


def _params(*semantics):
    return pltpu.CompilerParams(dimension_semantics=semantics, vmem_limit_bytes=VMEM_LIMIT)


def _ada_kernel(c_ref, w_ref, b_ref, o_ref):
    o_ref[...] = jnp.dot(c_ref[...], w_ref[...], preferred_element_type=F32) + b_ref[...]


def _ada(c_all, w_ada, b_ada):
    rows, d = c_all.shape
    n = w_ada.shape[1]
    return pl.pallas_call(
        _ada_kernel,
        grid=(n // d,),
        in_specs=[pl.BlockSpec((rows, d), lambda j: (0, 0)),
                  pl.BlockSpec((d, d), lambda j: (0, j)),
                  pl.BlockSpec((1, d), lambda j: (0, j))],
        out_specs=pl.BlockSpec((rows, d), lambda j: (0, j)),
        out_shape=jax.ShapeDtypeStruct((rows, n), F32),
        compiler_params=_params("arbitrary"),
        name="ada",
    )(c_all, w_ada, b_ada.reshape(1, n))


def _inproj_kernel(x_ref, sc_ref, sh_ref, wqkv_ref, wf_ref, wglu_ref, wgate_ref, bf_ref,
                   q_ref, k32_ref, v32_ref, kb_ref, vb_ref, logf_ref, glu_ref, gate_ref):
    nb, tt, d = x_ref.shape
    aw = q_ref.shape[-1]
    cw = glu_ref.shape[-1]
    nh = logf_ref.shape[-1]
    rows = nb * tt
    u = x_ref[...] * (1.0 + sc_ref[...]) + sh_ref[...]
    ub = u.reshape(rows, d).astype(BF16)

    def proj(w_ref, lo, hi):
        return jnp.dot(ub, w_ref[:, lo:hi], preferred_element_type=F32)

    q = proj(wqkv_ref, 0, aw) * (HEAD_DIM ** -0.5)
    q_ref[...] = q.reshape(nb, tt, aw).astype(BF16)
    k = proj(wqkv_ref, aw, 2 * aw).reshape(nb, tt, aw)
    k32_ref[...] = k
    kb_ref[...] = k.astype(BF16)
    v = proj(wqkv_ref, 2 * aw, 3 * aw).reshape(nb, tt, aw)
    v32_ref[...] = v
    vb_ref[...] = v.astype(BF16)

    f = jnp.dot(ub, wf_ref[...], preferred_element_type=F32)[:, :nh] + bf_ref[...]
    logf_ref[...] = _log_sigmoid(f).reshape(nb, tt, nh)

    glu = proj(wglu_ref, 0, cw) * _sigmoid(proj(wglu_ref, cw, 2 * cw))
    glu_ref[...] = glu.reshape(nb, tt, cw)

    gate = _sigmoid(jnp.dot(ub, wgate_ref[...], preferred_element_type=F32))
    gate_ref[...] = gate.reshape(nb, tt, gate.shape[-1]).astype(BF16)


def _inproj(x, sc, sh, wp, nb, tt):
    b, t, d = x.shape
    aw = wp["w_qkv"].shape[1] // 3
    cw = wp["w_glu"].shape[1] // 2
    gw = wp["w_gate"].shape[1]
    nh = wp["b_f"].shape[1]
    row = lambda width: pl.BlockSpec((nb, tt, width), lambda i, j: (i, j, 0))
    mod = pl.BlockSpec((nb, 1, d), lambda i, j: (i, 0, 0))
    sds = lambda width, dt: jax.ShapeDtypeStruct((b, t, width), dt)
    return pl.pallas_call(
        _inproj_kernel,
        grid=(b // nb, t // tt),
        in_specs=[row(d), mod, mod,
                  _const_spec(wp["w_qkv"].shape), _const_spec(wp["w_f"].shape),
                  _const_spec(wp["w_glu"].shape), _const_spec(wp["w_gate"].shape),
                  _const_spec(wp["b_f"].shape)],
        out_specs=[row(aw), row(aw), row(aw), row(aw), row(aw), row(nh), row(cw), row(gw)],
        out_shape=[sds(aw, BF16), sds(aw, F32), sds(aw, F32), sds(aw, BF16), sds(aw, BF16),
                   sds(nh, F32), sds(cw, F32), sds(gw, BF16)],
        compiler_params=_params("parallel", "parallel"),
        name="inproj",
    )(x, sc, sh, wp["w_qkv"], wp["w_f"], wp["w_glu"], wp["w_gate"], wp["b_f"])


def _split3(x):
    hi = x.astype(BF16)
    r1 = x - hi.astype(F32)
    mid = r1.astype(BF16)
    lo = (r1 - mid.astype(F32)).astype(BF16)
    return hi, mid, lo


def _cumsum_kernel(x_ref, o_ref):
    gb, r, _ = x_ref.shape
    li = lax.broadcasted_iota(jnp.int32, (LANES, LANES), 0)
    lj = lax.broadcasted_iota(jnp.int32, (LANES, LANES), 1)
    upper = jnp.where(li <= lj, 1.0, 0.0).astype(BF16)
    ri = lax.broadcasted_iota(jnp.int32, (r, r), 0)
    rj = lax.broadcasted_iota(jnp.int32, (r, r), 1)
    strict_lower = jnp.where(rj < ri, 1.0, 0.0).astype(BF16)
    for g in range(gb):
        x = x_ref[g]
        within = sum(jnp.dot(p, upper, preferred_element_type=F32) for p in _split3(x))
        total = jnp.broadcast_to(within[:, LANES - 1:LANES], (r, LANES))
        carry = sum(jnp.dot(strict_lower, p, preferred_element_type=F32) for p in _split3(total))
        o_ref[g] = within + carry


def _cumsum_time(logf_bht):
    b, h, l = logf_bht.shape
    tile = SUBLANES * LANES
    lp = -(-l // tile) * tile
    x = jnp.pad(logf_bht, ((0, 0), (0, 0), (0, lp - l))) if lp != l else logf_bht
    g, r = b * h, lp // LANES
    gb = SUBLANES
    out = pl.pallas_call(
        _cumsum_kernel,
        grid=(g // gb,),
        in_specs=[pl.BlockSpec((gb, r, LANES), lambda i: (i, 0, 0))],
        out_specs=pl.BlockSpec((gb, r, LANES), lambda i: (i, 0, 0)),
        out_shape=jax.ShapeDtypeStruct((g, r, LANES), F32),
        compiler_params=_params("parallel"),
        name="cumsum",
    )(x.reshape(g, r, LANES))
    return out.reshape(b, h, lp)[:, :, :l]


def _online_softmax_step(carry, s, v):
    m, l, acc = carry
    m_new = jnp.maximum(m, jnp.max(s, axis=-1, keepdims=True))
    alpha = jnp.exp(m - m_new)
    p = jnp.exp(s - m_new)
    l = alpha * l + jnp.sum(p, axis=-1, keepdims=True)
    acc = alpha * acc + jnp.dot(p.astype(BF16), v, preferred_element_type=F32)
    return m_new, l, acc


def _head_mask(shape, a):
    lane = lax.broadcasted_iota(jnp.int32, shape, 1)
    return (lane >= a * HEAD_DIM) & (lane < (a + 1) * HEAD_DIM)


def _scores(qa, k):
    return lax.dot_general(qa, k, (((1,), (1,)), ((), ())), preferred_element_type=F32)


def _attn_prompt_kernel(q_ref, k_ref, v_ref, f_ref, o_ref, *, tq):
    hp = pl.program_id(1)
    i = pl.program_id(2)
    q2 = q_ref[0]
    outs = []
    for a in range(HEADS_PER_BLOCK):
        qa = jnp.where(_head_mask(q2.shape, a), q2, jnp.zeros_like(q2))
        frow = HEADS_PER_BLOCK * hp + a

        def block(j, carry, masked):
            start = pl.multiple_of(j * tq, tq)
            s = _scores(qa, k_ref[0, pl.ds(start, tq), :])
            s = s - f_ref[0, pl.ds(frow, 1), pl.ds(start, tq)]
            if masked:
                qi = lax.broadcasted_iota(jnp.int32, s.shape, 0)
                ki = lax.broadcasted_iota(jnp.int32, s.shape, 1)
                s = jnp.where(ki <= qi, s, -jnp.inf)
            return _online_softmax_step(carry, s, v_ref[0, pl.ds(start, tq), :])

        init = (jnp.full((tq, 1), NEG_INIT, F32), jnp.zeros((tq, 1), F32),
                jnp.zeros((tq, LANES), F32))
        carry = lax.fori_loop(0, i, lambda j, c: block(j, c, False), init)
        _, l, acc = block(i, carry, True)
        outs.append(acc / l)
    out = outs[0]
    for a in range(1, HEADS_PER_BLOCK):
        out = jnp.where(_head_mask(out.shape, a), outs[a], out)
    o_ref[0] = out.astype(o_ref.dtype)


def _attn_prompt(q, kb, vb, f_bht, tq):
    b, t, aw = q.shape
    nh = f_bht.shape[1]
    return pl.pallas_call(
        functools.partial(_attn_prompt_kernel, tq=tq),
        grid=(b, aw // LANES, t // tq),
        in_specs=[pl.BlockSpec((1, tq, LANES), lambda bi, hp, i: (bi, i, hp)),
                  pl.BlockSpec((1, t, LANES), lambda bi, hp, i: (bi, 0, hp)),
                  pl.BlockSpec((1, t, LANES), lambda bi, hp, i: (bi, 0, hp)),
                  pl.BlockSpec((1, nh, t), lambda bi, hp, i: (bi, 0, 0))],
        out_specs=pl.BlockSpec((1, tq, LANES), lambda bi, hp, i: (bi, i, hp)),
        out_shape=jax.ShapeDtypeStruct((b, t, aw), BF16),
        compiler_params=_params("parallel", "parallel", "parallel"),
        name="attn_prompt",
    )(q, kb, vb, f_bht)


def _attn_cached_kernel(q_ref, ck_ref, cv_ref, kn_ref, vn_ref, fp_ref, fn_ref, o_ref, *, tk):
    hp = pl.program_id(1)
    q2 = q_ref[0]
    tq = q2.shape[0]
    n_past = ck_ref.shape[1] // tk
    outs = []
    for a in range(HEADS_PER_BLOCK):
        qa = jnp.where(_head_mask(q2.shape, a), q2, jnp.zeros_like(q2))
        frow = HEADS_PER_BLOCK * hp + a

        def past_block(j, carry):
            start = pl.multiple_of(j * tk, tk)
            s = _scores(qa, ck_ref[0, pl.ds(start, tk), :].astype(BF16))
            s = s - fp_ref[0, pl.ds(frow, 1), pl.ds(start, tk)]
            return _online_softmax_step(carry, s, cv_ref[0, pl.ds(start, tk), :].astype(BF16))

        init = (jnp.full((tq, 1), NEG_INIT, F32), jnp.zeros((tq, 1), F32),
                jnp.zeros((tq, LANES), F32))
        carry = lax.fori_loop(0, n_past, past_block, init)
        s = _scores(qa, kn_ref[0]) - fn_ref[0, pl.ds(frow, 1), :]
        qi = lax.broadcasted_iota(jnp.int32, s.shape, 0)
        ki = lax.broadcasted_iota(jnp.int32, s.shape, 1)
        s = jnp.where(ki <= qi, s, -jnp.inf)
        _, l, acc = _online_softmax_step(carry, s, vn_ref[0])
        outs.append(acc / l)
    out = outs[0]
    for a in range(1, HEADS_PER_BLOCK):
        out = jnp.where(_head_mask(out.shape, a), outs[a], out)
    o_ref[0] = out.astype(o_ref.dtype)


def _attn_cached(q, cache_k, cache_v, kb, vb, f_past, f_new, tk):
    b, t, aw = q.shape
    p = cache_k.shape[1]
    nh = f_past.shape[1]
    new = pl.BlockSpec((1, t, LANES), lambda bi, hp: (bi, 0, hp))
    past = pl.BlockSpec((1, p, LANES), lambda bi, hp: (bi, 0, hp))
    return pl.pallas_call(
        functools.partial(_attn_cached_kernel, tk=tk),
        grid=(b, aw // LANES),
        in_specs=[new, past, past, new, new,
                  pl.BlockSpec((1, nh, p), lambda bi, hp: (bi, 0, 0)),
                  pl.BlockSpec((1, nh, t), lambda bi, hp: (bi, 0, 0))],
        out_specs=new,
        out_shape=jax.ShapeDtypeStruct((b, t, aw), BF16),
        compiler_params=_params("parallel", "parallel"),
        name="attn_cached",
    )(q, cache_k, cache_v, kb, vb, f_past, f_new)


def _mix_kernel(glu_ref, hist_ref, attn_ref, gate_ref, x_ref, g1_ref,
                cw_ref, cb_ref, clg_ref, clb_ref, wap_ref, wcp_ref, wo_ref, l1g_ref, l1b_ref,
                x1_ref, ext_ref, hc_ref, *, alpha, row_chunk):
    nb, tt, cw = glu_ref.shape
    d = x_ref.shape[-1]
    taps = cw_ref.shape[0]
    n_hist = hist_ref.shape[1]
    rows = nb * tt

    @pl.when(pl.program_id(1) == 0)
    def _():
        ext_ref[:, CONV_HALO - n_hist:CONV_HALO, :] = hist_ref[...]

    ext_ref[:, CONV_HALO:CONV_HALO + tt, :] = glu_ref[...]

    first = CONV_HALO - (taps - 1)
    for b in range(nb):
        for c in range(tt // row_chunk):
            r0 = c * row_chunk
            acc = jnp.zeros((row_chunk, cw), F32)
            for k in range(taps):
                acc = acc + cw_ref[k:k + 1, :] * ext_ref[b, r0 + first + k:r0 + first + k + row_chunk, :]
            h = _layer_norm(acc + cb_ref[...], clg_ref[...], clb_ref[...])
            h = h * _sigmoid(h)
            hc_ref[b * tt + r0:b * tt + r0 + row_chunk, :] = h.astype(BF16)

    ext_ref[:, 0:CONV_HALO, :] = ext_ref[:, tt:tt + CONV_HALO, :]

    ya = jnp.dot(attn_ref[...].reshape(rows, attn_ref.shape[-1]), wap_ref[...],
                 preferred_element_type=F32)
    yb = jnp.dot(hc_ref[...], wcp_ref[...], preferred_element_type=F32)
    gate = gate_ref[...].reshape(rows, 2 * d)
    merged = gate[:, :d].astype(F32) * ya + gate[:, d:].astype(F32) * yb
    z = jnp.dot(merged.astype(BF16), wo_ref[...], preferred_element_type=F32).reshape(nb, tt, d)
    r = alpha * x_ref[...] + g1_ref[...] * z
    x1_ref[...] = _layer_norm(r, l1g_ref[...], l1b_ref[...])


def _mix(glu, hist, attn, gate, x, g1, wp, alpha, nb, tt):
    b, t, d = x.shape
    cw = glu.shape[-1]
    row_chunk = min(tt, 64)
    row = lambda width: pl.BlockSpec((nb, tt, width), lambda i, j: (i, j, 0))
    per_seq = lambda arr: pl.BlockSpec((nb,) + arr.shape[1:], lambda i, j: (i, 0, 0))
    consts = [wp["conv_w"], wp["conv_b"], wp["conv_ln_g"], wp["conv_ln_b"], wp["w_attn_proj"],
              wp["w_conv_proj"], wp["w_out"], wp["ln1_g"], wp["ln1_b"]]
    return pl.pallas_call(
        functools.partial(_mix_kernel, alpha=alpha, row_chunk=row_chunk),
        grid=(b // nb, t // tt),
        in_specs=[row(cw), per_seq(hist), row(attn.shape[-1]), row(gate.shape[-1]), row(d),
                  per_seq(g1)] + [_const_spec(c.shape) for c in consts],
        out_specs=row(d),
        out_shape=jax.ShapeDtypeStruct((b, t, d), F32),
        scratch_shapes=[pltpu.VMEM((nb, CONV_HALO + tt, cw), F32),
                        pltpu.VMEM((nb * tt, cw), BF16)],
        compiler_params=_params("parallel", "arbitrary"),
        name="mix",
    )(glu, hist, attn, gate, x, g1, *consts)


def _ffn_kernel(x1_ref, sc_ref, sh_ref, g2_ref, hist_ref, wua_ref, wuv_ref, fw_ref, fb_ref,
                wd_ref, l2g_ref, l2b_ref, y_ref, state_ref, ext_ref, *, alpha):
    nb, tt, d = x1_ref.shape
    dff = wua_ref.shape[1]
    taps = fw_ref.shape[0]
    n_hist = hist_ref.shape[1]
    rows = nb * tt

    @pl.when(pl.program_id(1) == 0)
    def _():
        ext_ref[:, FFN_HALO - n_hist:FFN_HALO, :] = hist_ref[...]

    x1 = x1_ref[...]
    u = x1 * (1.0 + sc_ref[...]) + sh_ref[...]
    ub = u.reshape(rows, d).astype(BF16)
    a2 = jnp.dot(ub, wua_ref[...], preferred_element_type=F32)
    ext_ref[:, FFN_HALO:FFN_HALO + tt, :] = a2.reshape(nb, tt, dff)
    state_ref[...] = ext_ref[:, FFN_HALO + tt - n_hist:FFN_HALO + tt, :]

    first = FFN_HALO - (taps - 1)
    conv = fb_ref[...].reshape(1, 1, dff)
    for k in range(taps):
        conv = conv + fw_ref[k:k + 1, :].reshape(1, 1, dff) * ext_ref[:, first + k:first + k + tt, :]
    v2 = jnp.dot(ub, wuv_ref[...], preferred_element_type=F32)
    conv = conv.reshape(rows, dff)
    h = conv * _sigmoid(conv) * v2

    ext_ref[:, 0:FFN_HALO, :] = ext_ref[:, tt:tt + FFN_HALO, :]

    z = jnp.dot(h.astype(BF16), wd_ref[...], preferred_element_type=F32).reshape(nb, tt, d)
    r = alpha * x1 + g2_ref[...] * z
    y_ref[...] = _layer_norm(r, l2g_ref[...], l2b_ref[...])


def _ffn(x1, sc, sh, g2, hist, wp, alpha, nb, tt):
    b, t, d = x1.shape
    dff = wp["w_up_a"].shape[1]
    n_hist = hist.shape[1]
    row = pl.BlockSpec((nb, tt, d), lambda i, j: (i, j, 0))
    per_seq = lambda arr: pl.BlockSpec((nb,) + arr.shape[1:], lambda i, j: (i, 0, 0))
    consts = [wp["w_up_a"], wp["w_up_v"], wp["ffn_conv_w"], wp["ffn_conv_b"], wp["w_down"],
              wp["ln2_g"], wp["ln2_b"]]
    return pl.pallas_call(
        functools.partial(_ffn_kernel, alpha=alpha),
        grid=(b // nb, t // tt),
        in_specs=[row, per_seq(sc), per_seq(sh), per_seq(g2), per_seq(hist)]
                 + [_const_spec(c.shape) for c in consts],
        out_specs=[row, pl.BlockSpec((nb, n_hist, dff), lambda i, j: (i, 0, 0))],
        out_shape=[jax.ShapeDtypeStruct((b, t, d), F32),
                   jax.ShapeDtypeStruct((b, n_hist, dff), F32)],
        scratch_shapes=[pltpu.VMEM((nb, FFN_HALO + tt, dff), F32)],
        compiler_params=_params("parallel", "arbitrary"),
        name="ffn",
    )(x1, sc, sh, g2, hist, *consts)


def _prepare_weights(w_in, b_f, conv_w, conv_b, conv_ln_g, conv_ln_b, w_attn_proj, w_conv_proj,
                     w_out, ln1_g, ln1_b, w_up, ffn_conv_w, ffn_conv_b, w_down, ln2_g, ln2_b):
    d = w_in.shape[0]
    aw = w_attn_proj.shape[0]
    cw = w_conv_proj.shape[0]
    nh = b_f.shape[0]
    dff = w_down.shape[0]
    s_v, s_f, s_glu = 3 * aw, 3 * aw + nh, 3 * aw + nh + 2 * cw
    row = lambda v: v.reshape(1, -1)
    return dict(
        w_qkv=w_in[:, :s_v].astype(BF16),
        w_f=jnp.pad(w_in[:, s_v:s_f], ((0, 0), (0, LANES - nh))).astype(BF16),
        w_glu=w_in[:, s_f:s_glu].astype(BF16),
        w_gate=w_in[:, s_glu:].astype(BF16),
        b_f=row(b_f),
        conv_w=conv_w, conv_b=row(conv_b), conv_ln_g=row(conv_ln_g), conv_ln_b=row(conv_ln_b),
        w_attn_proj=w_attn_proj.astype(BF16), w_conv_proj=w_conv_proj.astype(BF16),
        w_out=w_out.astype(BF16), ln1_g=row(ln1_g), ln1_b=row(ln1_b),
        w_up_a=w_up[:, :dff].astype(BF16), w_up_v=w_up[:, dff:].astype(BF16),
        ffn_conv_w=ffn_conv_w, ffn_conv_b=row(ffn_conv_b),
        w_down=w_down.astype(BF16), ln2_g=row(ln2_g), ln2_b=row(ln2_b),
    )


def _trunk_layer(x, mod, wp, alpha, past, conv_hist, ffn_hist, nb, tt, tq):
    b, t, d = x.shape
    sh1, sc1, g1, sh2, sc2, g2 = mod
    q, k32, v32, kb, vb, logf, glu, gate = _inproj(x, sc1, sh1, wp, nb, tt)

    logf_bht = jnp.transpose(logf, (0, 2, 1))
    if past is None:
        f_bht = _cumsum_time(logf_bht)
        attn = _attn_prompt(q, kb, vb, f_bht, tq)
    else:
        cache_k, cache_v, cache_logf = past
        p = cache_k.shape[1]
        f_all = _cumsum_time(jnp.concatenate([jnp.transpose(cache_logf, (0, 2, 1)), logf_bht], axis=2))
        attn = _attn_cached(q, cache_k, cache_v, kb, vb, f_all[:, :, :p], f_all[:, :, p:], tq)

    x1 = _mix(glu, conv_hist, attn, gate, x, g1, wp, alpha, nb, tt)
    y, ffn_state = _ffn(x1, sc2, sh2, g2, ffn_hist, wp, alpha, nb, tt)
    conv_state = glu[:, t - conv_hist.shape[1]:, :]
    return y, k32, v32, logf, conv_state, ffn_state


def kernel(x_prompt, x_sample, c_prompt, c_sample, cache_k, cache_v, cache_logf, state_conv, state_ffn_conv, w_ada, b_ada, w_in, b_f, conv_w, conv_b, conv_ln_g, conv_ln_b, w_attn_proj, w_conv_proj, w_out, ln1_g, ln1_b, w_up, ffn_conv_w, ffn_conv_b, w_down, ln2_g, ln2_b):
    depth = w_ada.shape[0]
    bp, tp, d = x_prompt.shape
    bs, ts, _ = x_sample.shape
    nh = b_f.shape[1]
    alpha = float((2 * depth) ** 0.25)
    prompt_tile = min(tp, 512)
    prompt_tq = min(tp, 256)

    y_p, y_s = x_prompt, x_sample
    c_all = jnp.concatenate([c_prompt, c_sample], axis=0)
    outs = [[] for _ in range(10)]
    for l in range(depth):
        wp = _prepare_weights(w_in[l], b_f[l], conv_w[l], conv_b[l], conv_ln_g[l], conv_ln_b[l],
                              w_attn_proj[l], w_conv_proj[l], w_out[l], ln1_g[l], ln1_b[l],
                              w_up[l], ffn_conv_w[l], ffn_conv_b[l], w_down[l], ln2_g[l], ln2_b[l])
        mod = _ada(c_all, w_ada[l], b_ada[l])
        mod_p = [mod[:bp, i * d:(i + 1) * d].reshape(bp, 1, d) for i in range(N_MOD)]
        mod_s = [mod[bp:, i * d:(i + 1) * d].reshape(bs, 1, d) for i in range(N_MOD)]

        zc = jnp.zeros((bp,) + state_conv.shape[2:], F32)
        zf = jnp.zeros((bp,) + state_ffn_conv.shape[2:], F32)
        y_p, kp, vp, fp, cp, ffp = _trunk_layer(
            y_p, mod_p, wp, alpha, None, zc, zf, nb=1, tt=prompt_tile, tq=prompt_tq)

        past_len = cache_k.shape[2]
        past = (cache_k[l].reshape(bs, past_len, -1), cache_v[l].reshape(bs, past_len, -1),
                cache_logf[l])
        y_s, ks, vs, fs, cs, ffs = _trunk_layer(
            y_s, mod_s, wp, alpha, past, state_conv[l], state_ffn_conv[l],
            nb=bs, tt=ts, tq=min(past_len, 512))

        heads = lambda a: a.reshape(a.shape[0], a.shape[1], nh, HEAD_DIM)
        for lst, val in zip(outs, (heads(kp), heads(vp), fp, cp, ffp,
                                   heads(ks), heads(vs), fs, cs, ffs)):
            lst.append(val)
    stacked = [jnp.stack(lst) for lst in outs]
    return (y_p, y_s, *stacked)
```

```python
import functools

import jax
import jax.numpy as jnp
from jax import lax
from jax.experimental import pallas as pl
from jax.experimental.pallas import tpu as pltpu

F32 = jnp.float32
BF16 = jnp.bfloat16

LN_EPS = 1e-5
N_MOD = 6
HEAD_DIM = 64
LANES = 128
SUBLANES = 8
HEADS_PER_BLOCK = LANES // HEAD_DIM
CONV_HALO = 32
FFN_HALO = 8
VMEM_LIMIT = 56 * 1024 * 1024
NEG_INIT = -1e30
LOG2E = 1.4426950408889634
N_SPLIT = 3


def _sigmoid(x):
    return 0.5 * (jnp.tanh(0.5 * x) + 1.0)


def _log_sigmoid(x):
    return jnp.minimum(x, 0.0) - jnp.log1p(jnp.exp(-jnp.abs(x)))


def _layer_norm(x, g, b):
    mu = jnp.mean(x, axis=-1, keepdims=True)
    xc = x - mu
    var = jnp.mean(xc * xc, axis=-1, keepdims=True)
    return xc * lax.rsqrt(var + LN_EPS) * g + b


def _const_spec(shape):
    zeros = (0,) * len(shape)
    return pl.BlockSpec(shape, lambda *_: zeros)


def _params(*semantics):
    return pltpu.CompilerParams(dimension_semantics=semantics, vmem_limit_bytes=VMEM_LIMIT)


def _ada_kernel(c_ref, w_ref, b_ref, o_ref):
    o_ref[...] = jnp.dot(c_ref[...], w_ref[...], preferred_element_type=F32) + b_ref[...]


def _ada(c_all, w_ada, b_ada):
    rows, d = c_all.shape
    n = w_ada.shape[1]
    return pl.pallas_call(
        _ada_kernel,
        grid=(n // d,),
        in_specs=[pl.BlockSpec((rows, d), lambda j: (0, 0)),
                  pl.BlockSpec((d, d), lambda j: (0, j)),
                  pl.BlockSpec((1, d), lambda j: (0, j))],
        out_specs=pl.BlockSpec((rows, d), lambda j: (0, j)),
        out_shape=jax.ShapeDtypeStruct((rows, n), F32),
        compiler_params=_params("arbitrary"),
        name="ada",
    )(c_all, w_ada, b_ada.reshape(1, n))


def _inproj_kernel(x_ref, sc_ref, sh_ref, wqkv_ref, wf_ref, wglu_ref, wgate_ref, bf_ref,
                   q_ref, k32_ref, v32_ref, kb_ref, vb_ref, logf_ref, glu_ref, gate_ref,
                   *, transposed):
    nb, tt, d = x_ref.shape
    aw = k32_ref.shape[-1]
    cw = glu_ref.shape[-1]
    nh = logf_ref.shape[-1]
    rows = nb * tt
    u = x_ref[...] * (1.0 + sc_ref[...]) + sh_ref[...]
    ub = u.reshape(rows, d).astype(BF16)

    def proj(w_ref, lo, hi):
        return jnp.dot(ub, w_ref[:, lo:hi], preferred_element_type=F32)

    q = proj(wqkv_ref, 0, aw) * (HEAD_DIM ** -0.5 * LOG2E)
    k = proj(wqkv_ref, aw, 2 * aw)
    k32_ref[...] = k.reshape(nb, tt, aw)
    kb_ref[...] = k.reshape(nb, tt, aw).astype(BF16)
    v = proj(wqkv_ref, 2 * aw, 3 * aw)
    v32_ref[...] = v.reshape(nb, tt, aw)
    if transposed:
        q_ref[0] = q.T.astype(BF16)
        vb_ref[0] = v.T.astype(BF16)
    else:
        q_ref[...] = q.reshape(nb, tt, aw).astype(BF16)
        vb_ref[...] = v.reshape(nb, tt, aw).astype(BF16)

    f = jnp.dot(ub, wf_ref[...], preferred_element_type=F32)[:, :nh] + bf_ref[...]
    logf_ref[...] = _log_sigmoid(f).reshape(nb, tt, nh)

    glu = proj(wglu_ref, 0, cw) * _sigmoid(proj(wglu_ref, cw, 2 * cw))
    glu_ref[...] = glu.reshape(nb, tt, cw)

    gate = _sigmoid(jnp.dot(ub, wgate_ref[...], preferred_element_type=F32))
    gate_ref[...] = gate.reshape(nb, tt, gate.shape[-1]).astype(BF16)


def _inproj(x, sc, sh, wp, nb, tt, transposed):
    b, t, d = x.shape
    aw = wp["w_qkv"].shape[1] // 3
    cw = wp["w_glu"].shape[1] // 2
    gw = wp["w_gate"].shape[1]
    nh = wp["b_f"].shape[1]
    assert not transposed or nb == 1
    row = lambda width: pl.BlockSpec((nb, tt, width), lambda i, j: (i, j, 0))
    mod = pl.BlockSpec((nb, 1, d), lambda i, j: (i, 0, 0))
    sds = lambda width, dt: jax.ShapeDtypeStruct((b, t, width), dt)
    if transposed:
        qv_spec = pl.BlockSpec((1, aw, tt), lambda i, j: (i, 0, j))
        qv_shape = jax.ShapeDtypeStruct((b, aw, t), BF16)
    else:
        qv_spec, qv_shape = row(aw), sds(aw, BF16)
    return pl.pallas_call(
        functools.partial(_inproj_kernel, transposed=transposed),
        grid=(b // nb, t // tt),
        in_specs=[row(d), mod, mod,
                  _const_spec(wp["w_qkv"].shape), _const_spec(wp["w_f"].shape),
                  _const_spec(wp["w_glu"].shape), _const_spec(wp["w_gate"].shape),
                  _const_spec(wp["b_f"].shape)],
        out_specs=[qv_spec, row(aw), row(aw), row(aw), qv_spec, row(nh), row(cw), row(gw)],
        out_shape=[qv_shape, sds(aw, F32), sds(aw, F32), sds(aw, BF16), qv_shape,
                   sds(nh, F32), sds(cw, F32), sds(gw, BF16)],
        compiler_params=_params("parallel", "parallel"),
        name="inproj",
    )(x, sc, sh, wp["w_qkv"], wp["w_f"], wp["w_glu"], wp["w_gate"], wp["b_f"])


def _split3(x):
    hi = x.astype(BF16)
    r1 = x - hi.astype(F32)
    mid = r1.astype(BF16)
    lo = (r1 - mid.astype(F32)).astype(BF16)
    return hi, mid, lo


def _cumsum_kernel(x_ref, o_ref, piece_ref):
    gb, r, _ = x_ref.shape
    li = lax.broadcasted_iota(jnp.int32, (LANES, LANES), 0)
    lj = lax.broadcasted_iota(jnp.int32, (LANES, LANES), 1)
    upper = jnp.where(li <= lj, 1.0, 0.0).astype(BF16)
    ri = lax.broadcasted_iota(jnp.int32, (r, r), 0)
    rj = lax.broadcasted_iota(jnp.int32, (r, r), 1)
    strict_lower = jnp.where(rj < ri, 1.0, 0.0).astype(BF16)
    for g in range(gb):
        x = x_ref[g]
        within = sum(jnp.dot(p, upper, preferred_element_type=F32) for p in _split3(x))
        total = jnp.broadcast_to(within[:, LANES - 1:LANES], (r, LANES))
        carry = sum(jnp.dot(strict_lower, p, preferred_element_type=F32) for p in _split3(total))
        f = within + carry
        o_ref[g] = f
        for p, piece in enumerate(_split3(f * (-LOG2E))):
            piece_ref[p, g] = piece


def _cumsum_time(logf_bht):
    b, h, l = logf_bht.shape
    tile = 2 * SUBLANES * LANES
    lp = -(-l // tile) * tile
    x = jnp.pad(logf_bht, ((0, 0), (0, 0), (0, lp - l))) if lp != l else logf_bht
    g, r = b * h, lp // LANES
    gb = SUBLANES
    f, pieces = pl.pallas_call(
        _cumsum_kernel,
        grid=(g // gb,),
        in_specs=[pl.BlockSpec((gb, r, LANES), lambda i: (i, 0, 0))],
        out_specs=[pl.BlockSpec((gb, r, LANES), lambda i: (i, 0, 0)),
                   pl.BlockSpec((N_SPLIT, gb, r, LANES), lambda i: (0, i, 0, 0))],
        out_shape=[jax.ShapeDtypeStruct((g, r, LANES), F32),
                   jax.ShapeDtypeStruct((N_SPLIT, g, r, LANES), BF16)],
        compiler_params=_params("parallel"),
        name="cumsum",
    )(x.reshape(g, r, LANES))
    return f.reshape(b, h, lp)[:, :, :l], pieces.reshape(N_SPLIT, b, h, lp)[:, :, :, :l]


def _online_softmax_step(carry, s, v):
    m, l, acc = carry
    m_new = jnp.maximum(m, jnp.max(s, axis=-1, keepdims=True))
    alpha = jnp.exp2(m - m_new)
    p = jnp.exp2(s - m_new)
    l = alpha * l + jnp.sum(p, axis=-1, keepdims=True)
    acc = alpha * acc + jnp.dot(p.astype(BF16), v, preferred_element_type=F32)
    return m_new, l, acc


def _head_mask(shape, a):
    lane = lax.broadcasted_iota(jnp.int32, shape, 1)
    return (lane >= a * HEAD_DIM) & (lane < (a + 1) * HEAD_DIM)


def _scores(qa, k):
    return lax.dot_general(qa, k, (((1,), (1,)), ((), ())), preferred_element_type=F32)


def _attn_prompt_kernel(qt_ref, k_ref, kf_ref, vt_ref, o_ref, s_ref, acc_ref, *, tq, tk):
    hp = pl.program_id(1)
    i = pl.program_id(2)
    qt = qt_ref[0]
    zeros = jnp.zeros((HEAD_DIM, tq), BF16)
    row = lax.broadcasted_iota(jnp.int32, (LANES, tq), 0)
    q_aug = []
    for a in range(HEADS_PER_BLOCK):
        first = N_SPLIT * (HEADS_PER_BLOCK * hp + a)
        pick = jnp.where((row >= first) & (row < first + N_SPLIT), 1.0, 0.0).astype(BF16)
        parts = [qt[h * HEAD_DIM:(h + 1) * HEAD_DIM] if h == a else zeros
                 for h in range(HEADS_PER_BLOCK)]
        q_aug.append(jnp.concatenate(parts + [pick], axis=0))

    heads = range(HEADS_PER_BLOCK)
    n_diag = tq // tk

    def produce(slot, blk, diag):
        start = pl.multiple_of(blk * tk, tk)
        k_aug = jnp.concatenate([k_ref[0, pl.ds(start, tk), :], kf_ref[0, pl.ds(start, tk), :]],
                                axis=1)
        maxima = []
        for a in heads:
            st = jnp.dot(k_aug, q_aug[a], preferred_element_type=F32)
            if diag is not None:
                ki = lax.broadcasted_iota(jnp.int32, st.shape, 0) + diag * tk
                qi = lax.broadcasted_iota(jnp.int32, st.shape, 1)
                st = jnp.where(ki <= qi, st, -jnp.inf)
            s_ref[slot, a] = st
            maxima.append(jnp.max(st.reshape(tk // SUBLANES, SUBLANES, tq), axis=0))
        return maxima

    def consume(slot, blk, m_row, m_prev, l8):
        start = pl.multiple_of(blk * tk, tk)
        new_l8 = []
        for a in heads:
            alpha = jnp.exp2(m_prev[a] - m_row[a])
            pt = jnp.exp2(s_ref[slot, a] - m_row[a])
            new_l8.append(alpha * l8[a] + jnp.sum(pt.reshape(tk // SUBLANES, SUBLANES, tq), axis=0))
            vt = vt_ref[0, a * HEAD_DIM:(a + 1) * HEAD_DIM, pl.ds(start, tk)]
            acc_ref[a] = alpha * acc_ref[a] + jnp.dot(vt, pt.astype(BF16),
                                                      preferred_element_type=F32)
        return new_l8

    def step(state, cons, prod):
        m8, m_prev, l8 = state
        m_row = [jnp.max(m8[a], axis=0, keepdims=True) for a in heads]
        l8 = consume(*cons, m_row, m_prev, l8)
        if prod is not None:
            m8 = [jnp.maximum(m8[a], x) for a, x in zip(heads, produce(*prod))]
        return m8, m_row, l8

    for a in heads:
        acc_ref[a] = jnp.zeros((HEAD_DIM, tq), F32)
    first_diag = n_diag * i
    m8 = produce(0, first_diag, 0)
    state = (m8, [jnp.full((1, tq), NEG_INIT, F32) for _ in heads],
             [jnp.zeros((SUBLANES, tq), F32) for _ in heads])
    for dg in range(1, n_diag):
        state = step(state, ((dg - 1) % 2, first_diag + dg - 1), (dg % 2, first_diag + dg, dg))
    pending = (n_diag - 1) % 2
    assert pending == 1 and n_diag == 2

    def pair(t, carry):
        m8, m_prev, l8 = carry
        blk = 2 * t
        prev_blk = jnp.where(t == 0, first_diag + n_diag - 1, blk - 1)
        state = step((list(m8), list(m_prev), list(l8)), (1, prev_blk), (0, blk, None))
        state = step(state, (0, blk), (1, blk + 1, None))
        return tuple(tuple(x) for x in state)

    state = lax.fori_loop(0, i * n_diag // 2, pair, tuple(tuple(x) for x in state))
    last_blk = jnp.where(i == 0, first_diag + n_diag - 1, first_diag - 1)
    _, _, l8 = step(tuple(list(x) for x in state), (1, last_blk), None)
    out_t = jnp.concatenate([acc_ref[a] / jnp.sum(l8[a], axis=0, keepdims=True) for a in heads],
                            axis=0)
    o_ref[0] = out_t.T.astype(o_ref.dtype)


def _attn_prompt(qt, kb, kf, vt, tq):
    b, aw, t = qt.shape
    tk = tq // 2
    return pl.pallas_call(
        functools.partial(_attn_prompt_kernel, tq=tq, tk=tk),
        scratch_shapes=[pltpu.VMEM((2, HEADS_PER_BLOCK, tk, tq), F32),
                        pltpu.VMEM((HEADS_PER_BLOCK, HEAD_DIM, tq), F32)],
        grid=(b, aw // LANES, t // tq),
        in_specs=[pl.BlockSpec((1, LANES, tq), lambda bi, hp, i: (bi, hp, i)),
                  pl.BlockSpec((1, t, LANES), lambda bi, hp, i: (bi, 0, hp)),
                  pl.BlockSpec((1, t, LANES), lambda bi, hp, i: (bi, 0, 0)),
                  pl.BlockSpec((1, LANES, t), lambda bi, hp, i: (bi, hp, 0))],
        out_specs=pl.BlockSpec((1, tq, LANES), lambda bi, hp, i: (bi, i, hp)),
        out_shape=jax.ShapeDtypeStruct((b, t, aw), BF16),
        compiler_params=_params("parallel", "parallel", "parallel"),
        name="attn_prompt",
    )(qt, kb, kf, vt)


def _attn_cached_kernel(q_ref, ck_ref, cv_ref, kn_ref, vn_ref, fp_ref, fn_ref, o_ref, *, tk):
    hp = pl.program_id(1)
    q2 = q_ref[0]
    tq = q2.shape[0]
    n_past = ck_ref.shape[1] // tk
    outs = []
    for a in range(HEADS_PER_BLOCK):
        qa = jnp.where(_head_mask(q2.shape, a), q2, jnp.zeros_like(q2))
        frow = HEADS_PER_BLOCK * hp + a

        def past_block(j, carry):
            start = pl.multiple_of(j * tk, tk)
            s = _scores(qa, ck_ref[0, pl.ds(start, tk), :].astype(BF16))
            s = s - LOG2E * fp_ref[0, pl.ds(frow, 1), pl.ds(start, tk)]
            return _online_softmax_step(carry, s, cv_ref[0, pl.ds(start, tk), :].astype(BF16))

        init = (jnp.full((tq, 1), NEG_INIT, F32), jnp.zeros((tq, 1), F32),
                jnp.zeros((tq, LANES), F32))
        carry = lax.fori_loop(0, n_past, past_block, init)
        s = _scores(qa, kn_ref[0]) - LOG2E * fn_ref[0, pl.ds(frow, 1), :]
        qi = lax.broadcasted_iota(jnp.int32, s.shape, 0)
        ki = lax.broadcasted_iota(jnp.int32, s.shape, 1)
        s = jnp.where(ki <= qi, s, -jnp.inf)
        _, l, acc = _online_softmax_step(carry, s, vn_ref[0])
        outs.append(acc / l)
    out = outs[0]
    for a in range(1, HEADS_PER_BLOCK):
        out = jnp.where(_head_mask(out.shape, a), outs[a], out)
    o_ref[0] = out.astype(o_ref.dtype)


def _attn_cached(q, cache_k, cache_v, kb, vb, f_past, f_new, tk):
    b, t, aw = q.shape
    p = cache_k.shape[1]
    nh = f_past.shape[1]
    new = pl.BlockSpec((1, t, LANES), lambda bi, hp: (bi, 0, hp))
    past = pl.BlockSpec((1, p, LANES), lambda bi, hp: (bi, 0, hp))
    return pl.pallas_call(
        functools.partial(_attn_cached_kernel, tk=tk),
        grid=(b, aw // LANES),
        in_specs=[new, past, past, new, new,
                  pl.BlockSpec((1, nh, p), lambda bi, hp: (bi, 0, 0)),
                  pl.BlockSpec((1, nh, t), lambda bi, hp: (bi, 0, 0))],
        out_specs=new,
        out_shape=jax.ShapeDtypeStruct((b, t, aw), BF16),
        compiler_params=_params("parallel", "parallel"),
        name="attn_cached",
    )(q, cache_k, cache_v, kb, vb, f_past, f_new)


def _mix_kernel(glu_ref, hist_ref, attn_ref, gate_ref, x_ref, g1_ref,
                cw_ref, cb_ref, clg_ref, clb_ref, wap_ref, wcp_ref, wo_ref, l1g_ref, l1b_ref,
                x1_ref, ext_ref, hc_ref, *, alpha, row_chunk):
    nb, tt, cw = glu_ref.shape
    d = x_ref.shape[-1]
    taps = cw_ref.shape[0]
    n_hist = hist_ref.shape[1]
    rows = nb * tt

    @pl.when(pl.program_id(1) == 0)
    def _():
        ext_ref[:, CONV_HALO - n_hist:CONV_HALO, :] = hist_ref[...]

    ext_ref[:, CONV_HALO:CONV_HALO + tt, :] = glu_ref[...]

    first = CONV_HALO - (taps - 1)
    for b in range(nb):
        for c in range(tt // row_chunk):
            r0 = c * row_chunk
            acc = jnp.zeros((row_chunk, cw), F32)
            for k in range(taps):
                acc = acc + cw_ref[k:k + 1, :] * ext_ref[b, r0 + first + k:r0 + first + k + row_chunk, :]
            h = _layer_norm(acc + cb_ref[...], clg_ref[...], clb_ref[...])
            h = h * _sigmoid(h)
            hc_ref[b * tt + r0:b * tt + r0 + row_chunk, :] = h.astype(BF16)

    ext_ref[:, 0:CONV_HALO, :] = ext_ref[:, tt:tt + CONV_HALO, :]

    ya = jnp.dot(attn_ref[...].reshape(rows, attn_ref.shape[-1]), wap_ref[...],
                 preferred_element_type=F32)
    yb = jnp.dot(hc_ref[...], wcp_ref[...], preferred_element_type=F32)
    gate = gate_ref[...].reshape(rows, 2 * d)
    merged = gate[:, :d].astype(F32) * ya + gate[:, d:].astype(F32) * yb
    z = jnp.dot(merged.astype(BF16), wo_ref[...], preferred_element_type=F32).reshape(nb, tt, d)
    r = alpha * x_ref[...] + g1_ref[...] * z
    x1_ref[...] = _layer_norm(r, l1g_ref[...], l1b_ref[...])


def _mix(glu, hist, attn, gate, x, g1, wp, alpha, nb, tt):
    b, t, d = x.shape
    cw = glu.shape[-1]
    row_chunk = min(tt, 64)
    row = lambda width: pl.BlockSpec((nb, tt, width), lambda i, j: (i, j, 0))
    per_seq = lambda arr: pl.BlockSpec((nb,) + arr.shape[1:], lambda i, j: (i, 0, 0))
    consts = [wp["conv_w"], wp["conv_b"], wp["conv_ln_g"], wp["conv_ln_b"], wp["w_attn_proj"],
              wp["w_conv_proj"], wp["w_out"], wp["ln1_g"], wp["ln1_b"]]
    return pl.pallas_call(
        functools.partial(_mix_kernel, alpha=alpha, row_chunk=row_chunk),
        grid=(b // nb, t // tt),
        in_specs=[row(cw), per_seq(hist), row(attn.shape[-1]), row(gate.shape[-1]), row(d),
                  per_seq(g1)] + [_const_spec(c.shape) for c in consts],
        out_specs=row(d),
        out_shape=jax.ShapeDtypeStruct((b, t, d), F32),
        scratch_shapes=[pltpu.VMEM((nb, CONV_HALO + tt, cw), F32),
                        pltpu.VMEM((nb * tt, cw), BF16)],
        compiler_params=_params("parallel", "arbitrary"),
        name="mix",
    )(glu, hist, attn, gate, x, g1, *consts)


def _ffn_kernel(x1_ref, sc_ref, sh_ref, g2_ref, hist_ref, wua_ref, wuv_ref, fw_ref, fb_ref,
                wd_ref, l2g_ref, l2b_ref, y_ref, state_ref, ext_ref, *, alpha):
    nb, tt, d = x1_ref.shape
    dff = wua_ref.shape[1]
    taps = fw_ref.shape[0]
    n_hist = hist_ref.shape[1]
    rows = nb * tt

    @pl.when(pl.program_id(1) == 0)
    def _():
        ext_ref[:, FFN_HALO - n_hist:FFN_HALO, :] = hist_ref[...]

    x1 = x1_ref[...]
    u = x1 * (1.0 + sc_ref[...]) + sh_ref[...]
    ub = u.reshape(rows, d).astype(BF16)
    a2 = jnp.dot(ub, wua_ref[...], preferred_element_type=F32)
    ext_ref[:, FFN_HALO:FFN_HALO + tt, :] = a2.reshape(nb, tt, dff)
    state_ref[...] = ext_ref[:, FFN_HALO + tt - n_hist:FFN_HALO + tt, :]

    first = FFN_HALO - (taps - 1)
    conv = fb_ref[...].reshape(1, 1, dff)
    for k in range(taps):
        conv = conv + fw_ref[k:k + 1, :].reshape(1, 1, dff) * ext_ref[:, first + k:first + k + tt, :]
    v2 = jnp.dot(ub, wuv_ref[...], preferred_element_type=F32)
    conv = conv.reshape(rows, dff)
    h = conv * _sigmoid(conv) * v2

    ext_ref[:, 0:FFN_HALO, :] = ext_ref[:, tt:tt + FFN_HALO, :]

    z = jnp.dot(h.astype(BF16), wd_ref[...], preferred_element_type=F32).reshape(nb, tt, d)
    r = alpha * x1 + g2_ref[...] * z
    y_ref[...] = _layer_norm(r, l2g_ref[...], l2b_ref[...])


def _ffn(x1, sc, sh, g2, hist, wp, alpha, nb, tt):
    b, t, d = x1.shape
    dff = wp["w_up_a"].shape[1]
    n_hist = hist.shape[1]
    row = pl.BlockSpec((nb, tt, d), lambda i, j: (i, j, 0))
    per_seq = lambda arr: pl.BlockSpec((nb,) + arr.shape[1:], lambda i, j: (i, 0, 0))
    consts = [wp["w_up_a"], wp["w_up_v"], wp["ffn_conv_w"], wp["ffn_conv_b"], wp["w_down"],
              wp["ln2_g"], wp["ln2_b"]]
    return pl.pallas_call(
        functools.partial(_ffn_kernel, alpha=alpha),
        grid=(b // nb, t // tt),
        in_specs=[row, per_seq(sc), per_seq(sh), per_seq(g2), per_seq(hist)]
                 + [_const_spec(c.shape) for c in consts],
        out_specs=[row, pl.BlockSpec((nb, n_hist, dff), lambda i, j: (i, 0, 0))],
        out_shape=[jax.ShapeDtypeStruct((b, t, d), F32),
                   jax.ShapeDtypeStruct((b, n_hist, dff), F32)],
        scratch_shapes=[pltpu.VMEM((nb, FFN_HALO + tt, dff), F32)],
        compiler_params=_params("parallel", "arbitrary"),
        name="ffn",
    )(x1, sc, sh, g2, hist, *consts)


def _prepare_weights(w_in, b_f, conv_w, conv_b, conv_ln_g, conv_ln_b, w_attn_proj, w_conv_proj,
                     w_out, ln1_g, ln1_b, w_up, ffn_conv_w, ffn_conv_b, w_down, ln2_g, ln2_b):
    d = w_in.shape[0]
    aw = w_attn_proj.shape[0]
    cw = w_conv_proj.shape[0]
    nh = b_f.shape[0]
    dff = w_down.shape[0]
    s_v, s_f, s_glu = 3 * aw, 3 * aw + nh, 3 * aw + nh + 2 * cw
    row = lambda v: v.reshape(1, -1)
    return dict(
        w_qkv=w_in[:, :s_v].astype(BF16),
        w_f=jnp.pad(w_in[:, s_v:s_f], ((0, 0), (0, LANES - nh))).astype(BF16),
        w_glu=w_in[:, s_f:s_glu].astype(BF16),
        w_gate=w_in[:, s_glu:].astype(BF16),
        b_f=row(b_f),
        conv_w=conv_w, conv_b=row(conv_b), conv_ln_g=row(conv_ln_g), conv_ln_b=row(conv_ln_b),
        w_attn_proj=w_attn_proj.astype(BF16), w_conv_proj=w_conv_proj.astype(BF16),
        w_out=w_out.astype(BF16), ln1_g=row(ln1_g), ln1_b=row(ln1_b),
        w_up_a=w_up[:, :dff].astype(BF16), w_up_v=w_up[:, dff:].astype(BF16),
        ffn_conv_w=ffn_conv_w, ffn_conv_b=row(ffn_conv_b),
        w_down=w_down.astype(BF16), ln2_g=row(ln2_g), ln2_b=row(ln2_b),
    )


def _trunk_layer(x, mod, wp, alpha, past, conv_hist, ffn_hist, nb, tt, tq):
    b, t, d = x.shape
    sh1, sc1, g1, sh2, sc2, g2 = mod
    q, k32, v32, kb, vb, logf, glu, gate = _inproj(x, sc1, sh1, wp, nb, tt, transposed=past is None)

    logf_bht = jnp.transpose(logf, (0, 2, 1))
    if past is None:
        _, pieces = _cumsum_time(logf_bht)
        kf = jnp.transpose(pieces, (1, 3, 2, 0)).reshape(b, t, -1)
        kf = jnp.pad(kf, ((0, 0), (0, 0), (0, LANES - kf.shape[-1])))
        attn = _attn_prompt(q, kb, kf, vb, tq)
    else:
        cache_k, cache_v, cache_logf = past
        p = cache_k.shape[1]
        f_all, _ = _cumsum_time(
            jnp.concatenate([jnp.transpose(cache_logf, (0, 2, 1)), logf_bht], axis=2))
        attn = _attn_cached(q, cache_k, cache_v, kb, vb, f_all[:, :, :p], f_all[:, :, p:], tq)

    x1 = _mix(glu, conv_hist, attn, gate, x, g1, wp, alpha, nb, tt)
    y, ffn_state = _ffn(x1, sc2, sh2, g2, ffn_hist, wp, alpha, nb, tt)
    conv_state = glu[:, t - conv_hist.shape[1]:, :]
    return y, k32, v32, logf, conv_state, ffn_state


def kernel(x_prompt, x_sample, c_prompt, c_sample, cache_k, cache_v, cache_logf, state_conv, state_ffn_conv, w_ada, b_ada, w_in, b_f, conv_w, conv_b, conv_ln_g, conv_ln_b, w_attn_proj, w_conv_proj, w_out, ln1_g, ln1_b, w_up, ffn_conv_w, ffn_conv_b, w_down, ln2_g, ln2_b):
    depth = w_ada.shape[0]
    bp, tp, d = x_prompt.shape
    bs, ts, _ = x_sample.shape
    nh = b_f.shape[1]
    alpha = float((2 * depth) ** 0.25)
    prompt_tile = min(tp, 512)
    prompt_tq = min(tp, 512)

    y_p, y_s = x_prompt, x_sample
    c_all = jnp.concatenate([c_prompt, c_sample], axis=0)
    outs = [[] for _ in range(10)]
    for l in range(depth):
        wp = _prepare_weights(w_in[l], b_f[l], conv_w[l], conv_b[l], conv_ln_g[l], conv_ln_b[l],
                              w_attn_proj[l], w_conv_proj[l], w_out[l], ln1_g[l], ln1_b[l],
                              w_up[l], ffn_conv_w[l], ffn_conv_b[l], w_down[l], ln2_g[l], ln2_b[l])
        mod = _ada(c_all, w_ada[l], b_ada[l])
        mod_p = [mod[:bp, i * d:(i + 1) * d].reshape(bp, 1, d) for i in range(N_MOD)]
        mod_s = [mod[bp:, i * d:(i + 1) * d].reshape(bs, 1, d) for i in range(N_MOD)]

        zc = jnp.zeros((bp,) + state_conv.shape[2:], F32)
        zf = jnp.zeros((bp,) + state_ffn_conv.shape[2:], F32)
        y_p, kp, vp, fp, cp, ffp = _trunk_layer(
            y_p, mod_p, wp, alpha, None, zc, zf, nb=1, tt=prompt_tile, tq=prompt_tq)

        past_len = cache_k.shape[2]
        past = (cache_k[l].reshape(bs, past_len, -1), cache_v[l].reshape(bs, past_len, -1),
                cache_logf[l])
        y_s, ks, vs, fs, cs, ffs = _trunk_layer(
            y_s, mod_s, wp, alpha, past, state_conv[l], state_ffn_conv[l],
            nb=bs, tt=ts, tq=min(past_len, 512))

        heads = lambda a: a.reshape(a.shape[0], a.shape[1], nh, HEAD_DIM)
        for lst, val in zip(outs, (heads(kp), heads(vp), fp, cp, ffp,
                                   heads(ks), heads(vs), fs, cs, ffs)):
            lst.append(val)
    stacked = [jnp.stack(lst) for lst in outs]
    return (y_p, y_s, *stacked)
```

```python
import functools

import jax
import jax.numpy as jnp
from jax import lax
from jax.experimental import pallas as pl
from jax.experimental.pallas import tpu as pltpu

F32 = jnp.float32
BF16 = jnp.bfloat16

LN_EPS = 1e-5
N_MOD = 6
HEAD_DIM = 64
LANES = 128
SUBLANES = 8
HEADS_PER_BLOCK = LANES // HEAD_DIM
ATTN_HEAD_GROUP = 8
CONV_HALO = 32
FFN_HALO = 8
VMEM_LIMIT = 56 * 1024 * 1024
NEG_INIT = -1e30
LOG2E = 1.4426950408889634
N_SPLIT = 3


def _sigmoid(x):
    return 0.5 * (jnp.tanh(0.5 * x) + 1.0)


def _log_sigmoid(x):
    return jnp.minimum(x, 0.0) - jnp.log1p(jnp.exp(-jnp.abs(x)))


def _layer_norm(x, g, b):
    mu = jnp.mean(x, axis=-1, keepdims=True)
    xc = x - mu
    var = jnp.mean(xc * xc, axis=-1, keepdims=True)
    return xc * lax.rsqrt(var + LN_EPS) * g + b


def _const_spec(shape):
    zeros = (0,) * len(shape)
    return pl.BlockSpec(shape, lambda *_: zeros)


def _params(*semantics):
    return pltpu.CompilerParams(dimension_semantics=semantics, vmem_limit_bytes=VMEM_LIMIT)


def _ada_kernel(c_ref, w_ref, b_ref, o_ref):
    o_ref[...] = jnp.dot(c_ref[...], w_ref[...], preferred_element_type=F32) + b_ref[...]


def _ada(c_all, w_ada, b_ada):
    rows, d = c_all.shape
    n = w_ada.shape[1]
    return pl.pallas_call(
        _ada_kernel,
        grid=(n // d,),
        in_specs=[pl.BlockSpec((rows, d), lambda j: (0, 0)),
                  pl.BlockSpec((d, d), lambda j: (0, j)),
                  pl.BlockSpec((1, d), lambda j: (0, j))],
        out_specs=pl.BlockSpec((rows, d), lambda j: (0, j)),
        out_shape=jax.ShapeDtypeStruct((rows, n), F32),
        compiler_params=_params("arbitrary"),
        name="ada",
    )(c_all, w_ada, b_ada.reshape(1, n))


def _inproj_kernel(x_ref, sc_ref, sh_ref, wqkv_ref, wf_ref, wglu_ref, wgate_ref, bf_ref,
                   q_ref, k32_ref, v32_ref, kb_ref, vb_ref, logf_ref, glu_ref, gate_ref,
                   *, transposed):
    nb, tt, d = x_ref.shape
    aw = k32_ref.shape[-1]
    cw = glu_ref.shape[-1]
    nh = logf_ref.shape[-1]
    rows = nb * tt
    u = x_ref[...] * (1.0 + sc_ref[...]) + sh_ref[...]
    ub = u.reshape(rows, d).astype(BF16)

    def proj(w_ref, lo, hi):
        return jnp.dot(ub, w_ref[:, lo:hi], preferred_element_type=F32)

    q = proj(wqkv_ref, 0, aw) * (HEAD_DIM ** -0.5 * LOG2E)
    k = proj(wqkv_ref, aw, 2 * aw)
    k32_ref[...] = k.reshape(nb, tt, aw)
    kb_ref[...] = k.reshape(nb, tt, aw).astype(BF16)
    v = proj(wqkv_ref, 2 * aw, 3 * aw)
    v32_ref[...] = v.reshape(nb, tt, aw)
    if transposed:
        q_ref[0] = q.T.astype(BF16)
        vb_ref[0] = v.T.astype(BF16)
    else:
        q_ref[...] = q.reshape(nb, tt, aw).astype(BF16)
        vb_ref[...] = v.reshape(nb, tt, aw).astype(BF16)

    f = jnp.dot(ub, wf_ref[...], preferred_element_type=F32)[:, :nh] + bf_ref[...]
    logf_ref[...] = _log_sigmoid(f).reshape(nb, tt, nh)

    glu = proj(wglu_ref, 0, cw) * _sigmoid(proj(wglu_ref, cw, 2 * cw))
    glu_ref[...] = glu.reshape(nb, tt, cw)

    gate = _sigmoid(jnp.dot(ub, wgate_ref[...], preferred_element_type=F32))
    gate_ref[...] = gate.reshape(nb, tt, gate.shape[-1]).astype(BF16)


def _inproj(x, sc, sh, wp, nb, tt, transposed):
    b, t, d = x.shape
    aw = wp["w_qkv"].shape[1] // 3
    cw = wp["w_glu"].shape[1] // 2
    gw = wp["w_gate"].shape[1]
    nh = wp["b_f"].shape[1]
    assert not transposed or nb == 1
    row = lambda width: pl.BlockSpec((nb, tt, width), lambda i, j: (i, j, 0))
    mod = pl.BlockSpec((nb, 1, d), lambda i, j: (i, 0, 0))
    sds = lambda width, dt: jax.ShapeDtypeStruct((b, t, width), dt)
    if transposed:
        qv_spec = pl.BlockSpec((1, aw, tt), lambda i, j: (i, 0, j))
        qv_shape = jax.ShapeDtypeStruct((b, aw, t), BF16)
    else:
        qv_spec, qv_shape = row(aw), sds(aw, BF16)
    return pl.pallas_call(
        functools.partial(_inproj_kernel, transposed=transposed),
        grid=(b // nb, t // tt),
        in_specs=[row(d), mod, mod,
                  _const_spec(wp["w_qkv"].shape), _const_spec(wp["w_f"].shape),
                  _const_spec(wp["w_glu"].shape), _const_spec(wp["w_gate"].shape),
                  _const_spec(wp["b_f"].shape)],
        out_specs=[qv_spec, row(aw), row(aw), row(aw), qv_spec, row(nh), row(cw), row(gw)],
        out_shape=[qv_shape, sds(aw, F32), sds(aw, F32), sds(aw, BF16), qv_shape,
                   sds(nh, F32), sds(cw, F32), sds(gw, BF16)],
        compiler_params=_params("parallel", "parallel"),
        name="inproj",
    )(x, sc, sh, wp["w_qkv"], wp["w_f"], wp["w_glu"], wp["w_gate"], wp["b_f"])


def _split3(x):
    hi = x.astype(BF16)
    r1 = x - hi.astype(F32)
    mid = r1.astype(BF16)
    lo = (r1 - mid.astype(F32)).astype(BF16)
    return hi, mid, lo


def _cumsum_kernel(x_ref, o_ref, piece_ref):
    gb, r, _ = x_ref.shape
    li = lax.broadcasted_iota(jnp.int32, (LANES, LANES), 0)
    lj = lax.broadcasted_iota(jnp.int32, (LANES, LANES), 1)
    upper = jnp.where(li <= lj, 1.0, 0.0).astype(BF16)
    ri = lax.broadcasted_iota(jnp.int32, (r, r), 0)
    rj = lax.broadcasted_iota(jnp.int32, (r, r), 1)
    strict_lower = jnp.where(rj < ri, 1.0, 0.0).astype(BF16)
    for g in range(gb):
        x = x_ref[g]
        within = sum(jnp.dot(p, upper, preferred_element_type=F32) for p in _split3(x))
        total = jnp.broadcast_to(within[:, LANES - 1:LANES], (r, LANES))
        carry = sum(jnp.dot(strict_lower, p, preferred_element_type=F32) for p in _split3(total))
        f = within + carry
        o_ref[g] = f
        for p, piece in enumerate(_split3(f * (-LOG2E))):
            piece_ref[p, g] = piece


def _cumsum_time(logf_bht):
    b, h, l = logf_bht.shape
    tile = 2 * SUBLANES * LANES
    lp = -(-l // tile) * tile
    x = jnp.pad(logf_bht, ((0, 0), (0, 0), (0, lp - l))) if lp != l else logf_bht
    g, r = b * h, lp // LANES
    gb = SUBLANES
    f, pieces = pl.pallas_call(
        _cumsum_kernel,
        grid=(g // gb,),
        in_specs=[pl.BlockSpec((gb, r, LANES), lambda i: (i, 0, 0))],
        out_specs=[pl.BlockSpec((gb, r, LANES), lambda i: (i, 0, 0)),
                   pl.BlockSpec((N_SPLIT, gb, r, LANES), lambda i: (0, i, 0, 0))],
        out_shape=[jax.ShapeDtypeStruct((g, r, LANES), F32),
                   jax.ShapeDtypeStruct((N_SPLIT, g, r, LANES), BF16)],
        compiler_params=_params("parallel"),
        name="cumsum",
    )(x.reshape(g, r, LANES))
    return f.reshape(b, h, lp)[:, :, :l], pieces.reshape(N_SPLIT, b, h, lp)[:, :, :, :l]


def _online_softmax_step(carry, s, v):
    m, l, acc = carry
    m_new = jnp.maximum(m, jnp.max(s, axis=-1, keepdims=True))
    alpha = jnp.exp2(m - m_new)
    p = jnp.exp2(s - m_new)
    l = alpha * l + jnp.sum(p, axis=-1, keepdims=True)
    acc = alpha * acc + jnp.dot(p.astype(BF16), v, preferred_element_type=F32)
    return m_new, l, acc


def _head_mask(shape, a):
    lane = lax.broadcasted_iota(jnp.int32, shape, 1)
    return (lane >= a * HEAD_DIM) & (lane < (a + 1) * HEAD_DIM)


def _scores(qa, k):
    return lax.dot_general(qa, k, (((1,), (1,)), ((), ())), preferred_element_type=F32)


def _attn_prompt_kernel(qt_ref, k_ref, kf_ref, vt_ref, o_ref, s_ref, acc_ref, *, tq, tk):
    group = pl.program_id(1)
    i = pl.program_id(2)
    n_heads = qt_ref.shape[1] // HEAD_DIM
    heads = range(n_heads)
    zeros = jnp.zeros((HEAD_DIM, tq), BF16)
    row = lax.broadcasted_iota(jnp.int32, (LANES, tq), 0)
    q_aug = []
    for a in heads:
        first = N_SPLIT * (n_heads * group + a)
        pick = jnp.where((row >= first) & (row < first + N_SPLIT), 1.0, 0.0).astype(BF16)
        pair = a // HEADS_PER_BLOCK
        parts = [qt_ref[0, h * HEAD_DIM:(h + 1) * HEAD_DIM, :] if h == a else zeros
                 for h in range(pair * HEADS_PER_BLOCK, (pair + 1) * HEADS_PER_BLOCK)]
        q_aug.append(jnp.concatenate(parts + [pick], axis=0))

    n_diag = tq // tk

    def produce(slot, blk, diag):
        start = pl.multiple_of(blk * tk, tk)
        kf = kf_ref[0, pl.ds(start, tk), :]
        k_aug = [jnp.concatenate([k_ref[0, pl.ds(start, tk), p * LANES:(p + 1) * LANES], kf], axis=1)
                 for p in range(n_heads // HEADS_PER_BLOCK)]
        maxima = []
        for a in heads:
            st = jnp.dot(k_aug[a // HEADS_PER_BLOCK], q_aug[a], preferred_element_type=F32)
            if diag is not None:
                ki = lax.broadcasted_iota(jnp.int32, st.shape, 0) + diag * tk
                qi = lax.broadcasted_iota(jnp.int32, st.shape, 1)
                st = jnp.where(ki <= qi, st, -jnp.inf)
            s_ref[slot, a] = st
            maxima.append(jnp.max(st.reshape(tk // SUBLANES, SUBLANES, tq), axis=0))
        return maxima

    def consume(slot, blk, m_row, m_prev, l8):
        start = pl.multiple_of(blk * tk, tk)
        new_l8 = []
        for a in heads:
            alpha = jnp.exp2(m_prev[a] - m_row[a])
            pt = jnp.exp2(s_ref[slot, a] - m_row[a])
            new_l8.append(alpha * l8[a] + jnp.sum(pt.reshape(tk // SUBLANES, SUBLANES, tq), axis=0))
            vt = vt_ref[0, a * HEAD_DIM:(a + 1) * HEAD_DIM, pl.ds(start, tk)]
            acc_ref[a] = alpha * acc_ref[a] + jnp.dot(vt, pt.astype(BF16),
                                                      preferred_element_type=F32)
        return new_l8

    def step(state, cons, prod):
        m8, m_prev, l8 = state
        m_row = [jnp.max(m8[a], axis=0, keepdims=True) for a in heads]
        l8 = consume(*cons, m_row, m_prev, l8)
        if prod is not None:
            m8 = [jnp.maximum(m8[a], x) for a, x in zip(heads, produce(*prod))]
        return m8, m_row, l8

    for a in heads:
        acc_ref[a] = jnp.zeros((HEAD_DIM, tq), F32)
    first_diag = n_diag * i
    m8 = produce(0, first_diag, 0)
    state = (m8, [jnp.full((1, tq), NEG_INIT, F32) for _ in heads],
             [jnp.zeros((SUBLANES, tq), F32) for _ in heads])
    for dg in range(1, n_diag):
        state = step(state, ((dg - 1) % 2, first_diag + dg - 1), (dg % 2, first_diag + dg, dg))
    pending = (n_diag - 1) % 2
    assert pending == 1 and n_diag == 2

    def pair(t, carry):
        m8, m_prev, l8 = carry
        blk = 2 * t
        prev_blk = jnp.where(t == 0, first_diag + n_diag - 1, blk - 1)
        state = step((list(m8), list(m_prev), list(l8)), (1, prev_blk), (0, blk, None))
        state = step(state, (0, blk), (1, blk + 1, None))
        return tuple(tuple(x) for x in state)

    state = lax.fori_loop(0, i * n_diag // 2, pair, tuple(tuple(x) for x in state))
    last_blk = jnp.where(i == 0, first_diag + n_diag - 1, first_diag - 1)
    _, _, l8 = step(tuple(list(x) for x in state), (1, last_blk), None)
    out_t = jnp.concatenate([acc_ref[a] / jnp.sum(l8[a], axis=0, keepdims=True) for a in heads],
                            axis=0)
    o_ref[0] = out_t.T.astype(o_ref.dtype)


def _attn_prompt(qt, kb, kf, vt, tq):
    b, aw, t = qt.shape
    tk = tq // 2
    gw = ATTN_HEAD_GROUP * HEAD_DIM
    return pl.pallas_call(
        functools.partial(_attn_prompt_kernel, tq=tq, tk=tk),
        scratch_shapes=[pltpu.VMEM((2, ATTN_HEAD_GROUP, tk, tq), F32),
                        pltpu.VMEM((ATTN_HEAD_GROUP, HEAD_DIM, tq), F32)],
        grid=(b, aw // gw, t // tq),
        in_specs=[pl.BlockSpec((1, gw, tq), lambda bi, g, i: (bi, g, i)),
                  pl.BlockSpec((1, t, gw), lambda bi, g, i: (bi, 0, g)),
                  pl.BlockSpec((1, t, LANES), lambda bi, g, i: (bi, 0, 0)),
                  pl.BlockSpec((1, gw, t), lambda bi, g, i: (bi, g, 0))],
        out_specs=pl.BlockSpec((1, tq, gw), lambda bi, g, i: (bi, i, g)),
        out_shape=jax.ShapeDtypeStruct((b, t, aw), BF16),
        compiler_params=_params("parallel", "parallel", "parallel"),
        name="attn_prompt",
    )(qt, kb, kf, vt)


def _attn_cached_kernel(q_ref, ck_ref, cv_ref, kn_ref, vn_ref, fp_ref, fn_ref, o_ref, *, tk):
    hp = pl.program_id(1)
    q2 = q_ref[0]
    tq = q2.shape[0]
    n_past = ck_ref.shape[1] // tk
    outs = []
    for a in range(HEADS_PER_BLOCK):
        qa = jnp.where(_head_mask(q2.shape, a), q2, jnp.zeros_like(q2))
        frow = HEADS_PER_BLOCK * hp + a

        def past_block(j, carry):
            start = pl.multiple_of(j * tk, tk)
            s = _scores(qa, ck_ref[0, pl.ds(start, tk), :].astype(BF16))
            s = s - LOG2E * fp_ref[0, pl.ds(frow, 1), pl.ds(start, tk)]
            return _online_softmax_step(carry, s, cv_ref[0, pl.ds(start, tk), :].astype(BF16))

        init = (jnp.full((tq, 1), NEG_INIT, F32), jnp.zeros((tq, 1), F32),
                jnp.zeros((tq, LANES), F32))
        carry = lax.fori_loop(0, n_past, past_block, init)
        s = _scores(qa, kn_ref[0]) - LOG2E * fn_ref[0, pl.ds(frow, 1), :]
        qi = lax.broadcasted_iota(jnp.int32, s.shape, 0)
        ki = lax.broadcasted_iota(jnp.int32, s.shape, 1)
        s = jnp.where(ki <= qi, s, -jnp.inf)
        _, l, acc = _online_softmax_step(carry, s, vn_ref[0])
        outs.append(acc / l)
    out = outs[0]
    for a in range(1, HEADS_PER_BLOCK):
        out = jnp.where(_head_mask(out.shape, a), outs[a], out)
    o_ref[0] = out.astype(o_ref.dtype)


def _attn_cached(q, cache_k, cache_v, kb, vb, f_past, f_new, tk):
    b, t, aw = q.shape
    p = cache_k.shape[1]
    nh = f_past.shape[1]
    new = pl.BlockSpec((1, t, LANES), lambda bi, hp: (bi, 0, hp))
    past = pl.BlockSpec((1, p, LANES), lambda bi, hp: (bi, 0, hp))
    return pl.pallas_call(
        functools.partial(_attn_cached_kernel, tk=tk),
        grid=(b, aw // LANES),
        in_specs=[new, past, past, new, new,
                  pl.BlockSpec((1, nh, p), lambda bi, hp: (bi, 0, 0)),
                  pl.BlockSpec((1, nh, t), lambda bi, hp: (bi, 0, 0))],
        out_specs=new,
        out_shape=jax.ShapeDtypeStruct((b, t, aw), BF16),
        compiler_params=_params("parallel", "parallel"),
        name="attn_cached",
    )(q, cache_k, cache_v, kb, vb, f_past, f_new)


def _mix_kernel(glu_ref, hist_ref, attn_ref, gate_ref, x_ref, g1_ref,
                cw_ref, cb_ref, clg_ref, clb_ref, wap_ref, wcp_ref, wo_ref, l1g_ref, l1b_ref,
                x1_ref, sh_ref, hc_ref, *, alpha, row_chunk):
    nb, tt, cw = glu_ref.shape
    d = x_ref.shape[-1]
    taps = cw_ref.shape[0]
    n_hist = hist_ref.shape[1]
    rows = nb * tt

    ext_ref = sh_ref.at[0]
    length = CONV_HALO + tt

    @pl.when(pl.program_id(1) == 0)
    def _():
        ext_ref[:, 0:SUBLANES, :] = jnp.zeros((nb, SUBLANES, cw), F32)
        ext_ref[:, CONV_HALO - n_hist:CONV_HALO, :] = hist_ref[...]

    ext_ref[:, CONV_HALO:length, :] = glu_ref[...]
    for r in range(1, SUBLANES):
        sh_ref[r, :, 0:length - SUBLANES, :] = ext_ref[:, r:r + length - SUBLANES, :]

    first = CONV_HALO - (taps - 1)
    for b in range(nb):
        for c in range(tt // row_chunk):
            r0 = c * row_chunk
            acc = jnp.zeros((row_chunk, cw), F32)
            for k in range(taps):
                shift = (first + k) % SUBLANES
                base = r0 + first + k - shift
                acc = acc + cw_ref[k:k + 1, :] * sh_ref[shift, b, base:base + row_chunk, :]
            h = _layer_norm(acc + cb_ref[...], clg_ref[...], clb_ref[...])
            h = h * _sigmoid(h)
            hc_ref[b * tt + r0:b * tt + r0 + row_chunk, :] = h.astype(BF16)

    ext_ref[:, 0:CONV_HALO, :] = ext_ref[:, tt:tt + CONV_HALO, :]

    ya = jnp.dot(attn_ref[...].reshape(rows, attn_ref.shape[-1]), wap_ref[...],
                 preferred_element_type=F32)
    yb = jnp.dot(hc_ref[...], wcp_ref[...], preferred_element_type=F32)
    gate = gate_ref[...].reshape(rows, 2 * d)
    merged = gate[:, :d].astype(F32) * ya + gate[:, d:].astype(F32) * yb
    z = jnp.dot(merged.astype(BF16), wo_ref[...], preferred_element_type=F32).reshape(nb, tt, d)
    r = alpha * x_ref[...] + g1_ref[...] * z
    x1_ref[...] = _layer_norm(r, l1g_ref[...], l1b_ref[...])


def _mix(glu, hist, attn, gate, x, g1, wp, alpha, nb, tt):
    b, t, d = x.shape
    cw = glu.shape[-1]
    row_chunk = min(tt, 64)
    row = lambda width: pl.BlockSpec((nb, tt, width), lambda i, j: (i, j, 0))
    per_seq = lambda arr: pl.BlockSpec((nb,) + arr.shape[1:], lambda i, j: (i, 0, 0))
    consts = [wp["conv_w"], wp["conv_b"], wp["conv_ln_g"], wp["conv_ln_b"], wp["w_attn_proj"],
              wp["w_conv_proj"], wp["w_out"], wp["ln1_g"], wp["ln1_b"]]
    return pl.pallas_call(
        functools.partial(_mix_kernel, alpha=alpha, row_chunk=row_chunk),
        grid=(b // nb, t // tt),
        in_specs=[row(cw), per_seq(hist), row(attn.shape[-1]), row(gate.shape[-1]), row(d),
                  per_seq(g1)] + [_const_spec(c.shape) for c in consts],
        out_specs=row(d),
        out_shape=jax.ShapeDtypeStruct((b, t, d), F32),
        scratch_shapes=[pltpu.VMEM((SUBLANES, nb, CONV_HALO + tt, cw), F32),
                        pltpu.VMEM((nb * tt, cw), BF16)],
        compiler_params=_params("parallel", "arbitrary"),
        name="mix",
    )(glu, hist, attn, gate, x, g1, *consts)


def _ffn_kernel(x1_ref, sc_ref, sh_ref, g2_ref, hist_ref, wua_ref, wuv_ref, fw_ref, fb_ref,
                wd_ref, l2g_ref, l2b_ref, y_ref, state_ref, ext_ref, *, alpha):
    nb, tt, d = x1_ref.shape
    dff = wua_ref.shape[1]
    taps = fw_ref.shape[0]
    n_hist = hist_ref.shape[1]
    rows = nb * tt

    @pl.when(pl.program_id(1) == 0)
    def _():
        ext_ref[:, FFN_HALO - n_hist:FFN_HALO, :] = hist_ref[...]

    x1 = x1_ref[...]
    u = x1 * (1.0 + sc_ref[...]) + sh_ref[...]
    ub = u.reshape(rows, d).astype(BF16)
    a2 = jnp.dot(ub, wua_ref[...], preferred_element_type=F32)
    ext_ref[:, FFN_HALO:FFN_HALO + tt, :] = a2.reshape(nb, tt, dff)
    state_ref[...] = ext_ref[:, FFN_HALO + tt - n_hist:FFN_HALO + tt, :]

    first = FFN_HALO - (taps - 1)
    conv = fb_ref[...].reshape(1, 1, dff)
    for k in range(taps):
        conv = conv + fw_ref[k:k + 1, :].reshape(1, 1, dff) * ext_ref[:, first + k:first + k + tt, :]
    v2 = jnp.dot(ub, wuv_ref[...], preferred_element_type=F32)
    conv = conv.reshape(rows, dff)
    h = conv * _sigmoid(conv) * v2

    ext_ref[:, 0:FFN_HALO, :] = ext_ref[:, tt:tt + FFN_HALO, :]

    z = jnp.dot(h.astype(BF16), wd_ref[...], preferred_element_type=F32).reshape(nb, tt, d)
    r = alpha * x1 + g2_ref[...] * z
    y_ref[...] = _layer_norm(r, l2g_ref[...], l2b_ref[...])


def _ffn(x1, sc, sh, g2, hist, wp, alpha, nb, tt):
    b, t, d = x1.shape
    dff = wp["w_up_a"].shape[1]
    n_hist = hist.shape[1]
    row = pl.BlockSpec((nb, tt, d), lambda i, j: (i, j, 0))
    per_seq = lambda arr: pl.BlockSpec((nb,) + arr.shape[1:], lambda i, j: (i, 0, 0))
    consts = [wp["w_up_a"], wp["w_up_v"], wp["ffn_conv_w"], wp["ffn_conv_b"], wp["w_down"],
              wp["ln2_g"], wp["ln2_b"]]
    return pl.pallas_call(
        functools.partial(_ffn_kernel, alpha=alpha),
        grid=(b // nb, t // tt),
        in_specs=[row, per_seq(sc), per_seq(sh), per_seq(g2), per_seq(hist)]
                 + [_const_spec(c.shape) for c in consts],
        out_specs=[row, pl.BlockSpec((nb, n_hist, dff), lambda i, j: (i, 0, 0))],
        out_shape=[jax.ShapeDtypeStruct((b, t, d), F32),
                   jax.ShapeDtypeStruct((b, n_hist, dff), F32)],
        scratch_shapes=[pltpu.VMEM((nb, FFN_HALO + tt, dff), F32)],
        compiler_params=_params("parallel", "arbitrary"),
        name="ffn",
    )(x1, sc, sh, g2, hist, *consts)


def _prepare_weights(w_in, b_f, conv_w, conv_b, conv_ln_g, conv_ln_b, w_attn_proj, w_conv_proj,
                     w_out, ln1_g, ln1_b, w_up, ffn_conv_w, ffn_conv_b, w_down, ln2_g, ln2_b):
    d = w_in.shape[0]
    aw = w_attn_proj.shape[0]
    cw = w_conv_proj.shape[0]
    nh = b_f.shape[0]
    dff = w_down.shape[0]
    s_v, s_f, s_glu = 3 * aw, 3 * aw + nh, 3 * aw + nh + 2 * cw
    row = lambda v: v.reshape(1, -1)
    return dict(
        w_qkv=w_in[:, :s_v].astype(BF16),
        w_f=jnp.pad(w_in[:, s_v:s_f], ((0, 0), (0, LANES - nh))).astype(BF16),
        w_glu=w_in[:, s_f:s_glu].astype(BF16),
        w_gate=w_in[:, s_glu:].astype(BF16),
        b_f=row(b_f),
        conv_w=conv_w, conv_b=row(conv_b), conv_ln_g=row(conv_ln_g), conv_ln_b=row(conv_ln_b),
        w_attn_proj=w_attn_proj.astype(BF16), w_conv_proj=w_conv_proj.astype(BF16),
        w_out=w_out.astype(BF16), ln1_g=row(ln1_g), ln1_b=row(ln1_b),
        w_up_a=w_up[:, :dff].astype(BF16), w_up_v=w_up[:, dff:].astype(BF16),
        ffn_conv_w=ffn_conv_w, ffn_conv_b=row(ffn_conv_b),
        w_down=w_down.astype(BF16), ln2_g=row(ln2_g), ln2_b=row(ln2_b),
    )


def _trunk_layer(x, mod, wp, alpha, past, conv_hist, ffn_hist, nb, tt, tq):
    b, t, d = x.shape
    sh1, sc1, g1, sh2, sc2, g2 = mod
    q, k32, v32, kb, vb, logf, glu, gate = _inproj(x, sc1, sh1, wp, nb, tt, transposed=past is None)

    logf_bht = jnp.transpose(logf, (0, 2, 1))
    if past is None:
        _, pieces = _cumsum_time(logf_bht)
        kf = jnp.transpose(pieces, (1, 3, 2, 0)).reshape(b, t, -1)
        kf = jnp.pad(kf, ((0, 0), (0, 0), (0, LANES - kf.shape[-1])))
        attn = _attn_prompt(q, kb, kf, vb, tq)
    else:
        cache_k, cache_v, cache_logf = past
        p = cache_k.shape[1]
        f_all, _ = _cumsum_time(
            jnp.concatenate([jnp.transpose(cache_logf, (0, 2, 1)), logf_bht], axis=2))
        attn = _attn_cached(q, cache_k, cache_v, kb, vb, f_all[:, :, :p], f_all[:, :, p:], tq)

    x1 = _mix(glu, conv_hist, attn, gate, x, g1, wp, alpha, nb, tt)
    y, ffn_state = _ffn(x1, sc2, sh2, g2, ffn_hist, wp, alpha, nb, tt)
    conv_state = glu[:, t - conv_hist.shape[1]:, :]
    return y, k32, v32, logf, conv_state, ffn_state


def kernel(x_prompt, x_sample, c_prompt, c_sample, cache_k, cache_v, cache_logf, state_conv, state_ffn_conv, w_ada, b_ada, w_in, b_f, conv_w, conv_b, conv_ln_g, conv_ln_b, w_attn_proj, w_conv_proj, w_out, ln1_g, ln1_b, w_up, ffn_conv_w, ffn_conv_b, w_down, ln2_g, ln2_b):
    depth = w_ada.shape[0]
    bp, tp, d = x_prompt.shape
    bs, ts, _ = x_sample.shape
    nh = b_f.shape[1]
    alpha = float((2 * depth) ** 0.25)
    prompt_tile = min(tp, 512)
    prompt_tq = min(tp, 512)

    y_p, y_s = x_prompt, x_sample
    c_all = jnp.concatenate([c_prompt, c_sample], axis=0)
    outs = [[] for _ in range(10)]
    for l in range(depth):
        wp = _prepare_weights(w_in[l], b_f[l], conv_w[l], conv_b[l], conv_ln_g[l], conv_ln_b[l],
                              w_attn_proj[l], w_conv_proj[l], w_out[l], ln1_g[l], ln1_b[l],
                              w_up[l], ffn_conv_w[l], ffn_conv_b[l], w_down[l], ln2_g[l], ln2_b[l])
        mod = _ada(c_all, w_ada[l], b_ada[l])
        mod_p = [mod[:bp, i * d:(i + 1) * d].reshape(bp, 1, d) for i in range(N_MOD)]
        mod_s = [mod[bp:, i * d:(i + 1) * d].reshape(bs, 1, d) for i in range(N_MOD)]

        zc = jnp.zeros((bp,) + state_conv.shape[2:], F32)
        zf = jnp.zeros((bp,) + state_ffn_conv.shape[2:], F32)
        y_p, kp, vp, fp, cp, ffp = _trunk_layer(
            y_p, mod_p, wp, alpha, None, zc, zf, nb=1, tt=prompt_tile, tq=prompt_tq)

        past_len = cache_k.shape[2]
        past = (cache_k[l].reshape(bs, past_len, -1), cache_v[l].reshape(bs, past_len, -1),
                cache_logf[l])
        y_s, ks, vs, fs, cs, ffs = _trunk_layer(
            y_s, mod_s, wp, alpha, past, state_conv[l], state_ffn_conv[l],
            nb=bs, tt=ts, tq=min(past_len, 512))

        heads = lambda a: a.reshape(a.shape[0], a.shape[1], nh, HEAD_DIM)
        for lst, val in zip(outs, (heads(kp), heads(vp), fp, cp, ffp,
                                   heads(ks), heads(vs), fs, cs, ffs)):
            lst.append(val)
    stacked = [jnp.stack(lst) for lst in outs]
    return (y_p, y_s, *stacked)
```

```python
import functools

import jax
import jax.numpy as jnp
from jax import lax
from jax.experimental import pallas as pl
from jax.experimental.pallas import tpu as pltpu

F32 = jnp.float32
BF16 = jnp.bfloat16

LN_EPS = 1e-5
N_MOD = 6
HEAD_DIM = 64
LANES = 128
SUBLANES = 8
HEADS_PER_BLOCK = LANES // HEAD_DIM
ATTN_HEAD_GROUP = 8
CONV_HALO = 32
FFN_HALO = 8
VMEM_LIMIT = 56 * 1024 * 1024
NEG_INIT = -1e30
LOG2E = 1.4426950408889634
N_SPLIT = 3
ONES_ROWS = 16


def _sigmoid(x):
    return 0.5 * (jnp.tanh(0.5 * x) + 1.0)


def _log_sigmoid(x):
    return jnp.minimum(x, 0.0) - jnp.log1p(jnp.exp(-jnp.abs(x)))


def _layer_norm(x, g, b):
    mu = jnp.mean(x, axis=-1, keepdims=True)
    xc = x - mu
    var = jnp.mean(xc * xc, axis=-1, keepdims=True)
    return xc * lax.rsqrt(var + LN_EPS) * g + b


def _const_spec(shape):
    zeros = (0,) * len(shape)
    return pl.BlockSpec(shape, lambda *_: zeros)


def _params(*semantics):
    return pltpu.CompilerParams(dimension_semantics=semantics, vmem_limit_bytes=VMEM_LIMIT)


def _ada_kernel(c_ref, w_ref, b_ref, o_ref):
    o_ref[...] = jnp.dot(c_ref[...], w_ref[...], preferred_element_type=F32) + b_ref[...]


def _ada(c_all, w_ada, b_ada):
    rows, d = c_all.shape
    n = w_ada.shape[1]
    return pl.pallas_call(
        _ada_kernel,
        grid=(n // d,),
        in_specs=[pl.BlockSpec((rows, d), lambda j: (0, 0)),
                  pl.BlockSpec((d, d), lambda j: (0, j)),
                  pl.BlockSpec((1, d), lambda j: (0, j))],
        out_specs=pl.BlockSpec((rows, d), lambda j: (0, j)),
        out_shape=jax.ShapeDtypeStruct((rows, n), F32),
        compiler_params=_params("arbitrary"),
        name="ada",
    )(c_all, w_ada, b_ada.reshape(1, n))


def _inproj_kernel(x_ref, sc_ref, sh_ref, wqkv_ref, wf_ref, wglu_ref, wgate_ref, bf_ref,
                   q_ref, k32_ref, v32_ref, kb_ref, vb_ref, logf_ref, glu_ref, gate_ref,
                   *, transposed):
    nb, tt, d = x_ref.shape
    aw = k32_ref.shape[-1]
    cw = glu_ref.shape[-1]
    nh = logf_ref.shape[-1]
    rows = nb * tt
    u = x_ref[...] * (1.0 + sc_ref[...]) + sh_ref[...]
    ub = u.reshape(rows, d).astype(BF16)

    def proj(w_ref, lo, hi):
        return jnp.dot(ub, w_ref[:, lo:hi], preferred_element_type=F32)

    q = proj(wqkv_ref, 0, aw) * (HEAD_DIM ** -0.5 * LOG2E)
    k = proj(wqkv_ref, aw, 2 * aw)
    k32_ref[...] = k.reshape(nb, tt, aw)
    kb_ref[...] = k.reshape(nb, tt, aw).astype(BF16)
    v = proj(wqkv_ref, 2 * aw, 3 * aw)
    v32_ref[...] = v.reshape(nb, tt, aw)
    if transposed:
        q_ref[0] = q.T.astype(BF16)
        vb_ref[0] = v.T.astype(BF16)
    else:
        q_ref[...] = q.reshape(nb, tt, aw).astype(BF16)
        vb_ref[...] = v.reshape(nb, tt, aw).astype(BF16)

    f = jnp.dot(ub, wf_ref[...], preferred_element_type=F32)[:, :nh] + bf_ref[...]
    logf_ref[...] = _log_sigmoid(f).reshape(nb, tt, nh)

    glu = proj(wglu_ref, 0, cw) * _sigmoid(proj(wglu_ref, cw, 2 * cw))
    glu_ref[...] = glu.reshape(nb, tt, cw)

    gate = _sigmoid(jnp.dot(ub, wgate_ref[...], preferred_element_type=F32))
    gate_ref[...] = gate.reshape(nb, tt, gate.shape[-1]).astype(BF16)


def _inproj(x, sc, sh, wp, nb, tt, transposed):
    b, t, d = x.shape
    aw = wp["w_qkv"].shape[1] // 3
    cw = wp["w_glu"].shape[1] // 2
    gw = wp["w_gate"].shape[1]
    nh = wp["b_f"].shape[1]
    assert not transposed or nb == 1
    row = lambda width: pl.BlockSpec((nb, tt, width), lambda i, j: (i, j, 0))
    mod = pl.BlockSpec((nb, 1, d), lambda i, j: (i, 0, 0))
    sds = lambda width, dt: jax.ShapeDtypeStruct((b, t, width), dt)
    if transposed:
        qv_spec = pl.BlockSpec((1, aw, tt), lambda i, j: (i, 0, j))
        qv_shape = jax.ShapeDtypeStruct((b, aw, t), BF16)
    else:
        qv_spec, qv_shape = row(aw), sds(aw, BF16)
    return pl.pallas_call(
        functools.partial(_inproj_kernel, transposed=transposed),
        grid=(b // nb, t // tt),
        in_specs=[row(d), mod, mod,
                  _const_spec(wp["w_qkv"].shape), _const_spec(wp["w_f"].shape),
                  _const_spec(wp["w_glu"].shape), _const_spec(wp["w_gate"].shape),
                  _const_spec(wp["b_f"].shape)],
        out_specs=[qv_spec, row(aw), row(aw), row(aw), qv_spec, row(nh), row(cw), row(gw)],
        out_shape=[qv_shape, sds(aw, F32), sds(aw, F32), sds(aw, BF16), qv_shape,
                   sds(nh, F32), sds(cw, F32), sds(gw, BF16)],
        compiler_params=_params("parallel", "parallel"),
        name="inproj",
    )(x, sc, sh, wp["w_qkv"], wp["w_f"], wp["w_glu"], wp["w_gate"], wp["b_f"])


def _split3(x):
    hi = x.astype(BF16)
    r1 = x - hi.astype(F32)
    mid = r1.astype(BF16)
    lo = (r1 - mid.astype(F32)).astype(BF16)
    return hi, mid, lo


def _cumsum_kernel(x_ref, o_ref, piece_ref):
    gb, r, _ = x_ref.shape
    li = lax.broadcasted_iota(jnp.int32, (LANES, LANES), 0)
    lj = lax.broadcasted_iota(jnp.int32, (LANES, LANES), 1)
    upper = jnp.where(li <= lj, 1.0, 0.0).astype(BF16)
    ri = lax.broadcasted_iota(jnp.int32, (r, r), 0)
    rj = lax.broadcasted_iota(jnp.int32, (r, r), 1)
    strict_lower = jnp.where(rj < ri, 1.0, 0.0).astype(BF16)
    for g in range(gb):
        x = x_ref[g]
        within = sum(jnp.dot(p, upper, preferred_element_type=F32) for p in _split3(x))
        total = jnp.broadcast_to(within[:, LANES - 1:LANES], (r, LANES))
        carry = sum(jnp.dot(strict_lower, p, preferred_element_type=F32) for p in _split3(total))
        f = within + carry
        o_ref[g] = f
        for p, piece in enumerate(_split3(f * (-LOG2E))):
            piece_ref[p, g] = piece


def _cumsum_time(logf_bht):
    b, h, l = logf_bht.shape
    tile = 2 * SUBLANES * LANES
    lp = -(-l // tile) * tile
    x = jnp.pad(logf_bht, ((0, 0), (0, 0), (0, lp - l))) if lp != l else logf_bht
    g, r = b * h, lp // LANES
    gb = SUBLANES
    f, pieces = pl.pallas_call(
        _cumsum_kernel,
        grid=(g // gb,),
        in_specs=[pl.BlockSpec((gb, r, LANES), lambda i: (i, 0, 0))],
        out_specs=[pl.BlockSpec((gb, r, LANES), lambda i: (i, 0, 0)),
                   pl.BlockSpec((N_SPLIT, gb, r, LANES), lambda i: (0, i, 0, 0))],
        out_shape=[jax.ShapeDtypeStruct((g, r, LANES), F32),
                   jax.ShapeDtypeStruct((N_SPLIT, g, r, LANES), BF16)],
        compiler_params=_params("parallel"),
        name="cumsum",
    )(x.reshape(g, r, LANES))
    return f.reshape(b, h, lp)[:, :, :l], pieces.reshape(N_SPLIT, b, h, lp)[:, :, :, :l]


def _scores(qa, k):
    return lax.dot_general(qa, k, (((1,), (1,)), ((), ())), preferred_element_type=F32)


def _attn_prompt_kernel(qt_ref, k_ref, kf_ref, vt_ref, o_ref, s_ref, acc_ref, *, tq, tk):
    group = pl.program_id(1)
    i = pl.program_id(2)
    n_heads = qt_ref.shape[1] // HEAD_DIM
    heads = range(n_heads)
    zeros = jnp.zeros((HEAD_DIM, tq), BF16)
    row = lax.broadcasted_iota(jnp.int32, (LANES, tq), 0)
    q_aug = []
    for a in heads:
        first = N_SPLIT * (n_heads * group + a)
        pick = jnp.where((row >= first) & (row < first + N_SPLIT), 1.0, 0.0).astype(BF16)
        pair = a // HEADS_PER_BLOCK
        parts = [qt_ref[0, h * HEAD_DIM:(h + 1) * HEAD_DIM, :] if h == a else zeros
                 for h in range(pair * HEADS_PER_BLOCK, (pair + 1) * HEADS_PER_BLOCK)]
        q_aug.append(jnp.concatenate(parts + [pick], axis=0))

    n_diag = tq // tk

    def produce(slot, blk, diag):
        start = pl.multiple_of(blk * tk, tk)
        kf = kf_ref[0, pl.ds(start, tk), :]
        k_aug = [jnp.concatenate([k_ref[0, pl.ds(start, tk), p * LANES:(p + 1) * LANES], kf], axis=1)
                 for p in range(n_heads // HEADS_PER_BLOCK)]
        maxima = []
        for a in heads:
            st = jnp.dot(k_aug[a // HEADS_PER_BLOCK], q_aug[a], preferred_element_type=F32)
            if diag is not None:
                ki = lax.broadcasted_iota(jnp.int32, st.shape, 0) + diag * tk
                qi = lax.broadcasted_iota(jnp.int32, st.shape, 1)
                st = jnp.where(ki <= qi, st, -jnp.inf)
            s_ref[slot, a] = st
            maxima.append(jnp.max(st.reshape(tk // SUBLANES, SUBLANES, tq), axis=0))
        return maxima

    ones = jnp.ones((ONES_ROWS, tk), BF16)

    def consume(slot, blk, m_row, m_prev, l):
        start = pl.multiple_of(blk * tk, tk)
        new_l = []
        for a in heads:
            alpha = jnp.exp2(m_prev[a] - m_row[a])
            pt = jnp.exp2(s_ref[slot, a] - m_row[a])
            vt = vt_ref[0, a * HEAD_DIM:(a + 1) * HEAD_DIM, pl.ds(start, tk)]
            pv = jnp.dot(jnp.concatenate([vt, ones], axis=0), pt.astype(BF16),
                         preferred_element_type=F32)
            acc_ref[a] = alpha * acc_ref[a] + pv[:HEAD_DIM]
            new_l.append(alpha * l[a] + pv[HEAD_DIM:HEAD_DIM + 1])
        return new_l

    def step(state, cons, prod):
        m8, m_prev, l = state
        m_row = [jnp.max(m8[a], axis=0, keepdims=True) for a in heads]
        l = consume(*cons, m_row, m_prev, l)
        if prod is not None:
            m8 = [jnp.maximum(m8[a], x) for a, x in zip(heads, produce(*prod))]
        return m8, m_row, l

    for a in heads:
        acc_ref[a] = jnp.zeros((HEAD_DIM, tq), F32)
    first_diag = n_diag * i
    m8 = produce(0, first_diag, 0)
    state = (m8, [jnp.full((1, tq), NEG_INIT, F32) for _ in heads],
             [jnp.zeros((1, tq), F32) for _ in heads])
    for dg in range(1, n_diag):
        state = step(state, ((dg - 1) % 2, first_diag + dg - 1), (dg % 2, first_diag + dg, dg))
    pending = (n_diag - 1) % 2
    assert pending == 1 and n_diag == 2

    def pair(t, carry):
        blk = 2 * t
        prev_blk = jnp.where(t == 0, first_diag + n_diag - 1, blk - 1)
        state = step(tuple(list(x) for x in carry), (1, prev_blk), (0, blk, None))
        state = step(state, (0, blk), (1, blk + 1, None))
        return tuple(tuple(x) for x in state)

    state = lax.fori_loop(0, i * n_diag // 2, pair, tuple(tuple(x) for x in state))
    last_blk = jnp.where(i == 0, first_diag + n_diag - 1, first_diag - 1)
    _, _, l = step(tuple(list(x) for x in state), (1, last_blk), None)
    out_t = jnp.concatenate([acc_ref[a] / l[a] for a in heads], axis=0)
    o_ref[0] = out_t.T.astype(o_ref.dtype)


def _attn_prompt(qt, kb, kf, vt, tq):
    b, aw, t = qt.shape
    tk = tq // 2
    gw = ATTN_HEAD_GROUP * HEAD_DIM
    return pl.pallas_call(
        functools.partial(_attn_prompt_kernel, tq=tq, tk=tk),
        scratch_shapes=[pltpu.VMEM((2, ATTN_HEAD_GROUP, tk, tq), F32),
                        pltpu.VMEM((ATTN_HEAD_GROUP, HEAD_DIM, tq), F32)],
        grid=(b, aw // gw, t // tq),
        in_specs=[pl.BlockSpec((1, gw, tq), lambda bi, g, i: (bi, g, i)),
                  pl.BlockSpec((1, t, gw), lambda bi, g, i: (bi, 0, g)),
                  pl.BlockSpec((1, t, LANES), lambda bi, g, i: (bi, 0, 0)),
                  pl.BlockSpec((1, gw, t), lambda bi, g, i: (bi, g, 0))],
        out_specs=pl.BlockSpec((1, tq, gw), lambda bi, g, i: (bi, i, g)),
        out_shape=jax.ShapeDtypeStruct((b, t, aw), BF16),
        compiler_params=_params("parallel", "parallel", "parallel"),
        name="attn_prompt",
    )(qt, kb, kf, vt)


def _attn_cached_kernel(q_ref, ck_ref, cv_ref, kn_ref, vn_ref, fp_ref, fn_ref, o_ref,
                        qs_ref, m_ref, l_ref, acc_ref):
    j = pl.program_id(1)
    n_chunks = pl.num_programs(1) - 1
    n_heads, dh = ck_ref.shape[2], ck_ref.shape[3]
    tq = q_ref.shape[1]
    cols = n_heads * tq

    @pl.when(j == 0)
    def _():
        for h in range(n_heads):
            qs_ref[h * tq:(h + 1) * tq, :] = q_ref[0, :, h * dh:(h + 1) * dh]
        m_ref[...] = jnp.full(m_ref.shape, NEG_INIT, F32)
        l_ref[...] = jnp.zeros(l_ref.shape, F32)
        acc_ref[...] = jnp.zeros(acc_ref.shape, F32)

    src = lax.broadcasted_iota(jnp.int32, (LANES, cols), 0)
    dst = lax.broadcasted_iota(jnp.int32, (LANES, cols), 1)
    expand = jnp.where(src == dst // tq, 1.0, 0.0).astype(BF16)

    def update(k_ref, v_ref, f_ref, causal):
        n = k_ref.shape[1]
        rows = n * n_heads
        k = k_ref[0].reshape(rows, dh).astype(BF16)
        s = _scores(k, qs_ref[...]).reshape(n, n_heads, cols)
        bias = sum(jnp.dot(p, expand, preferred_element_type=F32) for p in _split3(f_ref[0]))
        s = s - LOG2E * bias.reshape(n, 1, cols)
        head = lax.broadcasted_iota(jnp.int32, (1, n_heads, cols), 1)
        col = lax.broadcasted_iota(jnp.int32, (1, n_heads, cols), 2)
        keep = head == col // tq
        if causal:
            pos = lax.broadcasted_iota(jnp.int32, (n, n_heads, cols), 0)
            keep = keep & (pos <= lax.broadcasted_iota(jnp.int32, (n, n_heads, cols), 2) % tq)
        s = jnp.where(keep, s, -jnp.inf)
        m_old = m_ref[...]
        m_new = jnp.maximum(m_old, jnp.max(s, axis=0))
        alpha = jnp.exp2(m_old - m_new)
        p = jnp.exp2(s - m_new[None])
        m_ref[...] = m_new
        l_ref[...] = alpha * l_ref[...] + jnp.sum(p, axis=0)
        v = v_ref[0].reshape(rows, dh).astype(BF16)
        pv = lax.dot_general(v, p.reshape(rows, cols).astype(BF16), (((0,), (0,)), ((), ())),
                             preferred_element_type=F32)
        acc_ref[...] = _diag(alpha, tq) * acc_ref[...] + pv

    @pl.when(j < n_chunks)
    def _():
        update(ck_ref, cv_ref, fp_ref, False)

    @pl.when(j == n_chunks)
    def _():
        update(kn_ref, vn_ref, fn_ref, True)
        out_t = acc_ref[...] / _diag(l_ref[...], tq)
        o_ref[0] = jnp.concatenate([out_t[:, h * tq:(h + 1) * tq].T for h in range(n_heads)],
                                   axis=1).astype(o_ref.dtype)


def _diag(x, tq):
    head = lax.broadcasted_iota(jnp.int32, x.shape, 0)
    col = lax.broadcasted_iota(jnp.int32, x.shape, 1)
    return jnp.sum(jnp.where(head == col // tq, x, 0.0), axis=0, keepdims=True)


def _attn_cached(q, cache_k, cache_v, k_new, v_new, f_t, chunk):
    b, t, aw = q.shape
    _, p, nh, dh = cache_k.shape
    n_chunks = p // chunk
    last = n_chunks - 1
    whole = lambda shape: pl.BlockSpec((1,) + shape, lambda bi, j: (bi,) + (0,) * len(shape))
    past = pl.BlockSpec((1, chunk, nh, dh), lambda bi, j: (bi, jnp.minimum(j, last), 0, 0))
    return pl.pallas_call(
        _attn_cached_kernel,
        grid=(b, n_chunks + 1),
        in_specs=[whole((t, aw)), past, past, whole((t, nh, dh)), whole((t, nh, dh)),
                  pl.BlockSpec((1, chunk, LANES), lambda bi, j: (bi, jnp.minimum(j, last), 0)),
                  pl.BlockSpec((1, t, LANES), lambda bi, j: (bi, p // t, 0))],
        out_specs=whole((t, aw)),
        out_shape=jax.ShapeDtypeStruct((b, t, aw), BF16),
        scratch_shapes=[pltpu.VMEM((nh * t, dh), BF16), pltpu.VMEM((nh, nh * t), F32),
                        pltpu.VMEM((nh, nh * t), F32), pltpu.VMEM((dh, nh * t), F32)],
        compiler_params=_params("parallel", "arbitrary"),
        name="attn_cached",
    )(q, cache_k, cache_v, k_new, v_new, f_t, f_t)


def _mix_kernel(glu_ref, hist_ref, attn_ref, gate_ref, x_ref, g1_ref,
                cw_ref, cb_ref, clg_ref, clb_ref, wap_ref, wcp_ref, wo_ref, l1g_ref, l1b_ref,
                x1_ref, sh_ref, hc_ref, *, alpha, row_chunk):
    nb, tt, cw = glu_ref.shape
    d = x_ref.shape[-1]
    taps = cw_ref.shape[0]
    n_hist = hist_ref.shape[1]
    rows = nb * tt

    ext_ref = sh_ref.at[0]
    length = CONV_HALO + tt

    @pl.when(pl.program_id(1) == 0)
    def _():
        ext_ref[:, 0:SUBLANES, :] = jnp.zeros((nb, SUBLANES, cw), F32)
        ext_ref[:, CONV_HALO - n_hist:CONV_HALO, :] = hist_ref[...]

    ext_ref[:, CONV_HALO:length, :] = glu_ref[...]
    for r in range(1, SUBLANES):
        sh_ref[r, :, 0:length - SUBLANES, :] = ext_ref[:, r:r + length - SUBLANES, :]

    first = CONV_HALO - (taps - 1)
    for b in range(nb):
        for c in range(tt // row_chunk):
            r0 = c * row_chunk
            acc = jnp.zeros((row_chunk, cw), F32)
            for k in range(taps):
                shift = (first + k) % SUBLANES
                base = r0 + first + k - shift
                acc = acc + cw_ref[k:k + 1, :] * sh_ref[shift, b, base:base + row_chunk, :]
            h = _layer_norm(acc + cb_ref[...], clg_ref[...], clb_ref[...])
            h = h * _sigmoid(h)
            hc_ref[b * tt + r0:b * tt + r0 + row_chunk, :] = h.astype(BF16)

    ext_ref[:, 0:CONV_HALO, :] = ext_ref[:, tt:tt + CONV_HALO, :]

    ya = jnp.dot(attn_ref[...].reshape(rows, attn_ref.shape[-1]), wap_ref[...],
                 preferred_element_type=F32)
    yb = jnp.dot(hc_ref[...], wcp_ref[...], preferred_element_type=F32)
    gate = gate_ref[...].reshape(rows, 2 * d)
    merged = gate[:, :d].astype(F32) * ya + gate[:, d:].astype(F32) * yb
    z = jnp.dot(merged.astype(BF16), wo_ref[...], preferred_element_type=F32).reshape(nb, tt, d)
    r = alpha * x_ref[...] + g1_ref[...] * z
    x1_ref[...] = _layer_norm(r, l1g_ref[...], l1b_ref[...])


def _mix(glu, hist, attn, gate, x, g1, wp, alpha, nb, tt):
    b, t, d = x.shape
    cw = glu.shape[-1]
    row_chunk = min(tt, 64)
    row = lambda width: pl.BlockSpec((nb, tt, width), lambda i, j: (i, j, 0))
    per_seq = lambda arr: pl.BlockSpec((nb,) + arr.shape[1:], lambda i, j: (i, 0, 0))
    consts = [wp["conv_w"], wp["conv_b"], wp["conv_ln_g"], wp["conv_ln_b"], wp["w_attn_proj"],
              wp["w_conv_proj"], wp["w_out"], wp["ln1_g"], wp["ln1_b"]]
    return pl.pallas_call(
        functools.partial(_mix_kernel, alpha=alpha, row_chunk=row_chunk),
        grid=(b // nb, t // tt),
        in_specs=[row(cw), per_seq(hist), row(attn.shape[-1]), row(gate.shape[-1]), row(d),
                  per_seq(g1)] + [_const_spec(c.shape) for c in consts],
        out_specs=row(d),
        out_shape=jax.ShapeDtypeStruct((b, t, d), F32),
        scratch_shapes=[pltpu.VMEM((SUBLANES, nb, CONV_HALO + tt, cw), F32),
                        pltpu.VMEM((nb * tt, cw), BF16)],
        compiler_params=_params("parallel", "arbitrary"),
        name="mix",
    )(glu, hist, attn, gate, x, g1, *consts)


def _ffn_kernel(x1_ref, sc_ref, sh_ref, g2_ref, hist_ref, wua_ref, wuv_ref, fw_ref, fb_ref,
                wd_ref, l2g_ref, l2b_ref, y_ref, state_ref, ext_ref, *, alpha):
    nb, tt, d = x1_ref.shape
    dff = wua_ref.shape[1]
    taps = fw_ref.shape[0]
    n_hist = hist_ref.shape[1]
    rows = nb * tt

    @pl.when(pl.program_id(1) == 0)
    def _():
        ext_ref[:, FFN_HALO - n_hist:FFN_HALO, :] = hist_ref[...]

    x1 = x1_ref[...]
    u = x1 * (1.0 + sc_ref[...]) + sh_ref[...]
    ub = u.reshape(rows, d).astype(BF16)
    a2 = jnp.dot(ub, wua_ref[...], preferred_element_type=F32)
    ext_ref[:, FFN_HALO:FFN_HALO + tt, :] = a2.reshape(nb, tt, dff)
    state_ref[...] = ext_ref[:, FFN_HALO + tt - n_hist:FFN_HALO + tt, :]

    first = FFN_HALO - (taps - 1)
    conv = fb_ref[...].reshape(1, 1, dff)
    for k in range(taps):
        conv = conv + fw_ref[k:k + 1, :].reshape(1, 1, dff) * ext_ref[:, first + k:first + k + tt, :]
    v2 = jnp.dot(ub, wuv_ref[...], preferred_element_type=F32)
    conv = conv.reshape(rows, dff)
    h = conv * _sigmoid(conv) * v2

    ext_ref[:, 0:FFN_HALO, :] = ext_ref[:, tt:tt + FFN_HALO, :]

    z = jnp.dot(h.astype(BF16), wd_ref[...], preferred_element_type=F32).reshape(nb, tt, d)
    r = alpha * x1 + g2_ref[...] * z
    y_ref[...] = _layer_norm(r, l2g_ref[...], l2b_ref[...])


def _ffn(x1, sc, sh, g2, hist, wp, alpha, nb, tt):
    b, t, d = x1.shape
    dff = wp["w_up_a"].shape[1]
    n_hist = hist.shape[1]
    row = pl.BlockSpec((nb, tt, d), lambda i, j: (i, j, 0))
    per_seq = lambda arr: pl.BlockSpec((nb,) + arr.shape[1:], lambda i, j: (i, 0, 0))
    consts = [wp["w_up_a"], wp["w_up_v"], wp["ffn_conv_w"], wp["ffn_conv_b"], wp["w_down"],
              wp["ln2_g"], wp["ln2_b"]]
    return pl.pallas_call(
        functools.partial(_ffn_kernel, alpha=alpha),
        grid=(b // nb, t // tt),
        in_specs=[row, per_seq(sc), per_seq(sh), per_seq(g2), per_seq(hist)]
                 + [_const_spec(c.shape) for c in consts],
        out_specs=[row, pl.BlockSpec((nb, n_hist, dff), lambda i, j: (i, 0, 0))],
        out_shape=[jax.ShapeDtypeStruct((b, t, d), F32),
                   jax.ShapeDtypeStruct((b, n_hist, dff), F32)],
        scratch_shapes=[pltpu.VMEM((nb, FFN_HALO + tt, dff), F32)],
        compiler_params=_params("parallel", "arbitrary"),
        name="ffn",
    )(x1, sc, sh, g2, hist, *consts)


def _prepare_weights(w_in, b_f, conv_w, conv_b, conv_ln_g, conv_ln_b, w_attn_proj, w_conv_proj,
                     w_out, ln1_g, ln1_b, w_up, ffn_conv_w, ffn_conv_b, w_down, ln2_g, ln2_b):
    d = w_in.shape[0]
    aw = w_attn_proj.shape[0]
    cw = w_conv_proj.shape[0]
    nh = b_f.shape[0]
    dff = w_down.shape[0]
    s_v, s_f, s_glu = 3 * aw, 3 * aw + nh, 3 * aw + nh + 2 * cw
    row = lambda v: v.reshape(1, -1)
    return dict(
        w_qkv=w_in[:, :s_v].astype(BF16),
        w_f=jnp.pad(w_in[:, s_v:s_f], ((0, 0), (0, LANES - nh))).astype(BF16),
        w_glu=w_in[:, s_f:s_glu].astype(BF16),
        w_gate=w_in[:, s_glu:].astype(BF16),
        b_f=row(b_f),
        conv_w=conv_w, conv_b=row(conv_b), conv_ln_g=row(conv_ln_g), conv_ln_b=row(conv_ln_b),
        w_attn_proj=w_attn_proj.astype(BF16), w_conv_proj=w_conv_proj.astype(BF16),
        w_out=w_out.astype(BF16), ln1_g=row(ln1_g), ln1_b=row(ln1_b),
        w_up_a=w_up[:, :dff].astype(BF16), w_up_v=w_up[:, dff:].astype(BF16),
        ffn_conv_w=ffn_conv_w, ffn_conv_b=row(ffn_conv_b),
        w_down=w_down.astype(BF16), ln2_g=row(ln2_g), ln2_b=row(ln2_b),
    )


def _trunk_layer(x, mod, wp, alpha, past, conv_hist, ffn_hist, nb, tt, tq):
    b, t, d = x.shape
    sh1, sc1, g1, sh2, sc2, g2 = mod
    q, k32, v32, kb, vb, logf, glu, gate = _inproj(x, sc1, sh1, wp, nb, tt, transposed=past is None)

    logf_bht = jnp.transpose(logf, (0, 2, 1))
    if past is None:
        _, pieces = _cumsum_time(logf_bht)
        kf = jnp.transpose(pieces, (1, 3, 2, 0)).reshape(b, t, -1)
        kf = jnp.pad(kf, ((0, 0), (0, 0), (0, LANES - kf.shape[-1])))
        attn = _attn_prompt(q, kb, kf, vb, tq)
    else:
        cache_k, cache_v, cache_logf = past
        nh, dh = cache_k.shape[2:]
        f_all, _ = _cumsum_time(
            jnp.concatenate([jnp.transpose(cache_logf, (0, 2, 1)), logf_bht], axis=2))
        f_t = jnp.pad(jnp.transpose(f_all, (0, 2, 1)), ((0, 0), (0, 0), (0, LANES - nh)))
        attn = _attn_cached(q, cache_k, cache_v, k32.reshape(b, t, nh, dh),
                            v32.reshape(b, t, nh, dh), f_t, tq)

    x1 = _mix(glu, conv_hist, attn, gate, x, g1, wp, alpha, nb, tt)
    y, ffn_state = _ffn(x1, sc2, sh2, g2, ffn_hist, wp, alpha, nb, tt)
    conv_state = glu[:, t - conv_hist.shape[1]:, :]
    return y, k32, v32, logf, conv_state, ffn_state


def kernel(x_prompt, x_sample, c_prompt, c_sample, cache_k, cache_v, cache_logf, state_conv, state_ffn_conv, w_ada, b_ada, w_in, b_f, conv_w, conv_b, conv_ln_g, conv_ln_b, w_attn_proj, w_conv_proj, w_out, ln1_g, ln1_b, w_up, ffn_conv_w, ffn_conv_b, w_down, ln2_g, ln2_b):
    depth = w_ada.shape[0]
    bp, tp, d = x_prompt.shape
    bs, ts, _ = x_sample.shape
    nh = b_f.shape[1]
    alpha = float((2 * depth) ** 0.25)
    prompt_tile = min(tp, 512)
    prompt_tq = min(tp, 512)

    y_p, y_s = x_prompt, x_sample
    c_all = jnp.concatenate([c_prompt, c_sample], axis=0)
    outs = [[] for _ in range(10)]
    for l in range(depth):
        wp = _prepare_weights(w_in[l], b_f[l], conv_w[l], conv_b[l], conv_ln_g[l], conv_ln_b[l],
                              w_attn_proj[l], w_conv_proj[l], w_out[l], ln1_g[l], ln1_b[l],
                              w_up[l], ffn_conv_w[l], ffn_conv_b[l], w_down[l], ln2_g[l], ln2_b[l])
        mod = _ada(c_all, w_ada[l], b_ada[l])
        mod_p = [mod[:bp, i * d:(i + 1) * d].reshape(bp, 1, d) for i in range(N_MOD)]
        mod_s = [mod[bp:, i * d:(i + 1) * d].reshape(bs, 1, d) for i in range(N_MOD)]

        zc = jnp.zeros((bp,) + state_conv.shape[2:], F32)
        zf = jnp.zeros((bp,) + state_ffn_conv.shape[2:], F32)
        y_p, kp, vp, fp, cp, ffp = _trunk_layer(
            y_p, mod_p, wp, alpha, None, zc, zf, nb=1, tt=prompt_tile, tq=prompt_tq)

        past_len = cache_k.shape[2]
        past = (cache_k[l], cache_v[l], cache_logf[l])
        y_s, ks, vs, fs, cs, ffs = _trunk_layer(
            y_s, mod_s, wp, alpha, past, state_conv[l], state_ffn_conv[l],
            nb=bs, tt=ts, tq=min(past_len, 512))

        heads = lambda a: a.reshape(a.shape[0], a.shape[1], nh, HEAD_DIM)
        for lst, val in zip(outs, (heads(kp), heads(vp), fp, cp, ffp,
                                   heads(ks), heads(vs), fs, cs, ffs)):
            lst.append(val)
    stacked = [jnp.stack(lst) for lst in outs]
    return (y_p, y_s, *stacked)
```

```python
import functools

import jax
import jax.numpy as jnp
from jax import lax
from jax.experimental import pallas as pl
from jax.experimental.pallas import tpu as pltpu

F32 = jnp.float32
BF16 = jnp.bfloat16

LN_EPS = 1e-5
N_MOD = 6
HEAD_DIM = 64
LANES = 128
SUBLANES = 8
HEADS_PER_BLOCK = LANES // HEAD_DIM
ATTN_HEAD_GROUP = 8
CONV_HALO = 32
FFN_HALO = 8
VMEM_LIMIT = 56 * 1024 * 1024
NEG_INIT = -1e30
LOG2E = 1.4426950408889634
N_SPLIT = 3
ONES_ROWS = 16


def _sigmoid(x):
    return 0.5 * (jnp.tanh(0.5 * x) + 1.0)


def _log_sigmoid(x):
    return jnp.minimum(x, 0.0) - jnp.log1p(jnp.exp(-jnp.abs(x)))


def _layer_norm(x, g, b):
    mu = jnp.mean(x, axis=-1, keepdims=True)
    xc = x - mu
    var = jnp.mean(xc * xc, axis=-1, keepdims=True)
    return xc * lax.rsqrt(var + LN_EPS) * g + b


def _const_spec(shape):
    zeros = (0,) * len(shape)
    return pl.BlockSpec(shape, lambda *_: zeros)


def _params(*semantics):
    return pltpu.CompilerParams(dimension_semantics=semantics, vmem_limit_bytes=VMEM_LIMIT)


def _ada_kernel(c_ref, w_ref, b_ref, o_ref):
    o_ref[...] = jnp.dot(c_ref[...], w_ref[...], preferred_element_type=F32) + b_ref[...]


def _ada(c_all, w_ada, b_ada):
    rows, d = c_all.shape
    n = w_ada.shape[1]
    return pl.pallas_call(
        _ada_kernel,
        grid=(n // d,),
        in_specs=[pl.BlockSpec((rows, d), lambda j: (0, 0)),
                  pl.BlockSpec((d, d), lambda j: (0, j)),
                  pl.BlockSpec((1, d), lambda j: (0, j))],
        out_specs=pl.BlockSpec((rows, d), lambda j: (0, j)),
        out_shape=jax.ShapeDtypeStruct((rows, n), F32),
        compiler_params=_params("arbitrary"),
        name="ada",
    )(c_all, w_ada, b_ada.reshape(1, n))


def _inproj_kernel(x_ref, sc_ref, sh_ref, wqkv_ref, wf_ref, wglu_ref, wgate_ref, bf_ref,
                   q_ref, k32_ref, v32_ref, kb_ref, vb_ref, logf_ref, glu_ref, gate_ref,
                   *, transposed):
    nb, tt, d = x_ref.shape
    aw = kb_ref.shape[-1]
    cw = glu_ref.shape[-1]
    rows = nb * tt
    u = x_ref[...] * (1.0 + sc_ref[...]) + sh_ref[...]
    ub = u.reshape(rows, d).astype(BF16)

    def proj(w_ref, lo, hi):
        return jnp.dot(ub, w_ref[:, lo:hi], preferred_element_type=F32)

    q = proj(wqkv_ref, 0, aw) * (HEAD_DIM ** -0.5 * LOG2E)
    k = proj(wqkv_ref, aw, 2 * aw)
    kb_ref[...] = k.reshape(nb, tt, aw).astype(BF16)
    v = proj(wqkv_ref, 2 * aw, 3 * aw)
    logf = _log_sigmoid(jnp.dot(ub, wf_ref[...], preferred_element_type=F32) + bf_ref[...])
    if transposed:
        vt = v.T
        q_ref[0] = q.T.astype(BF16)
        k32_ref[0] = k.T
        v32_ref[0] = vt
        vb_ref[0] = vt.astype(BF16)
        logf_ref[0] = logf.T[:logf_ref.shape[1]]
    else:
        q_ref[...] = q.reshape(nb, tt, aw).astype(BF16)
        k32_ref[...] = k.reshape(nb, tt, aw)
        v32_ref[...] = v.reshape(nb, tt, aw)
        vb_ref[...] = v.reshape(nb, tt, aw).astype(BF16)
        nh = logf_ref.shape[-1]
        logf_ref[...] = logf[:, :nh].reshape(nb, tt, nh)

    glu = proj(wglu_ref, 0, cw) * _sigmoid(proj(wglu_ref, cw, 2 * cw))
    glu_ref[...] = glu.reshape(nb, tt, cw)

    gate = _sigmoid(jnp.dot(ub, wgate_ref[...], preferred_element_type=F32))
    gate_ref[...] = gate.reshape(nb, tt, gate.shape[-1]).astype(BF16)


def _inproj(x, sc, sh, wp, nb, tt, transposed):
    b, t, d = x.shape
    aw = wp["w_qkv"].shape[1] // 3
    cw = wp["w_glu"].shape[1] // 2
    gw = wp["w_gate"].shape[1]
    nh = wp["n_heads"]
    assert not transposed or nb == 1
    row = lambda width: pl.BlockSpec((nb, tt, width), lambda i, j: (i, j, 0))
    mod = pl.BlockSpec((nb, 1, d), lambda i, j: (i, 0, 0))
    sds = lambda width, dt: jax.ShapeDtypeStruct((b, t, width), dt)
    if transposed:
        col = lambda width: pl.BlockSpec((1, width, tt), lambda i, j: (i, 0, j))
        sds_t = lambda width, dt: jax.ShapeDtypeStruct((b, width, t), dt)
    else:
        col, sds_t = row, sds
    return pl.pallas_call(
        functools.partial(_inproj_kernel, transposed=transposed),
        grid=(b // nb, t // tt),
        in_specs=[row(d), mod, mod,
                  _const_spec(wp["w_qkv"].shape), _const_spec(wp["w_f"].shape),
                  _const_spec(wp["w_glu"].shape), _const_spec(wp["w_gate"].shape),
                  _const_spec(wp["b_f"].shape)],
        out_specs=[col(aw), col(aw), col(aw), row(aw), col(aw), col(nh), row(cw), row(gw)],
        out_shape=[sds_t(aw, BF16), sds_t(aw, F32), sds_t(aw, F32), sds(aw, BF16), sds_t(aw, BF16),
                   sds_t(nh, F32), sds(cw, F32), sds(gw, BF16)],
        compiler_params=_params("parallel", "parallel"),
        name="inproj",
    )(x, sc, sh, wp["w_qkv"], wp["w_f"], wp["w_glu"], wp["w_gate"], wp["b_f"])


def _split3(x):
    hi = x.astype(BF16)
    r1 = x - hi.astype(F32)
    mid = r1.astype(BF16)
    lo = (r1 - mid.astype(F32)).astype(BF16)
    return hi, mid, lo


def _cumsum_kernel(x_ref, o_ref, piece_ref):
    gb, r, _ = x_ref.shape
    li = lax.broadcasted_iota(jnp.int32, (LANES, LANES), 0)
    lj = lax.broadcasted_iota(jnp.int32, (LANES, LANES), 1)
    upper = jnp.where(li <= lj, 1.0, 0.0).astype(BF16)
    ri = lax.broadcasted_iota(jnp.int32, (r, r), 0)
    rj = lax.broadcasted_iota(jnp.int32, (r, r), 1)
    strict_lower = jnp.where(rj < ri, 1.0, 0.0).astype(BF16)
    for g in range(gb):
        x = x_ref[g]
        within = sum(jnp.dot(p, upper, preferred_element_type=F32) for p in _split3(x))
        total = jnp.broadcast_to(within[:, LANES - 1:LANES], (r, LANES))
        carry = sum(jnp.dot(strict_lower, p, preferred_element_type=F32) for p in _split3(total))
        f = within + carry
        o_ref[g] = f
        for p, piece in enumerate(_split3(f * (-LOG2E))):
            piece_ref[p, g] = piece


def _cumsum_time(logf_bht):
    b, h, l = logf_bht.shape
    tile = 2 * SUBLANES * LANES
    lp = -(-l // tile) * tile
    x = jnp.pad(logf_bht, ((0, 0), (0, 0), (0, lp - l))) if lp != l else logf_bht
    g, r = b * h, lp // LANES
    gb = SUBLANES
    f, pieces = pl.pallas_call(
        _cumsum_kernel,
        grid=(g // gb,),
        in_specs=[pl.BlockSpec((gb, r, LANES), lambda i: (i, 0, 0))],
        out_specs=[pl.BlockSpec((gb, r, LANES), lambda i: (i, 0, 0)),
                   pl.BlockSpec((N_SPLIT, gb, r, LANES), lambda i: (0, i, 0, 0))],
        out_shape=[jax.ShapeDtypeStruct((g, r, LANES), F32),
                   jax.ShapeDtypeStruct((N_SPLIT, g, r, LANES), BF16)],
        compiler_params=_params("parallel"),
        name="cumsum",
    )(x.reshape(g, r, LANES))
    return f.reshape(b, h, lp)[:, :, :l], pieces.reshape(N_SPLIT, b, h, lp)[:, :, :, :l]


def _scores(qa, k):
    return lax.dot_general(qa, k, (((1,), (1,)), ((), ())), preferred_element_type=F32)


def _attn_prompt_kernel(qt_ref, k_ref, kf_ref, vt_ref, o_ref, s_ref, acc_ref, *, tq, tk):
    group = pl.program_id(1)
    i = pl.program_id(2)
    n_heads = qt_ref.shape[1] // HEAD_DIM
    heads = range(n_heads)
    zeros = jnp.zeros((HEAD_DIM, tq), BF16)
    row = lax.broadcasted_iota(jnp.int32, (LANES, tq), 0)
    q_aug = []
    for a in heads:
        first = N_SPLIT * (n_heads * group + a)
        pick = jnp.where((row >= first) & (row < first + N_SPLIT), 1.0, 0.0).astype(BF16)
        pair = a // HEADS_PER_BLOCK
        parts = [qt_ref[0, h * HEAD_DIM:(h + 1) * HEAD_DIM, :] if h == a else zeros
                 for h in range(pair * HEADS_PER_BLOCK, (pair + 1) * HEADS_PER_BLOCK)]
        q_aug.append(jnp.concatenate(parts + [pick], axis=0))

    n_diag = tq // tk

    def produce(slot, blk, diag):
        start = pl.multiple_of(blk * tk, tk)
        kf = kf_ref[0, pl.ds(start, tk), :]
        k_aug = [jnp.concatenate([k_ref[0, pl.ds(start, tk), p * LANES:(p + 1) * LANES], kf], axis=1)
                 for p in range(n_heads // HEADS_PER_BLOCK)]
        maxima = []
        for a in heads:
            st = jnp.dot(k_aug[a // HEADS_PER_BLOCK], q_aug[a], preferred_element_type=F32)
            if diag is not None:
                ki = lax.broadcasted_iota(jnp.int32, st.shape, 0) + diag * tk
                qi = lax.broadcasted_iota(jnp.int32, st.shape, 1)
                st = jnp.where(ki <= qi, st, -jnp.inf)
            s_ref[slot, a] = st
            maxima.append(jnp.max(st.reshape(tk // SUBLANES, SUBLANES, tq), axis=0))
        return maxima

    ones = jnp.ones((ONES_ROWS, tk), BF16)

    def consume(slot, blk, m_row, m_prev, l):
        start = pl.multiple_of(blk * tk, tk)
        new_l = []
        for a in heads:
            alpha = jnp.exp2(m_prev[a] - m_row[a])
            pt = jnp.exp2(s_ref[slot, a] - m_row[a])
            vt = vt_ref[0, a * HEAD_DIM:(a + 1) * HEAD_DIM, pl.ds(start, tk)]
            pv = jnp.dot(jnp.concatenate([vt, ones], axis=0), pt.astype(BF16),
                         preferred_element_type=F32)
            acc_ref[a] = alpha * acc_ref[a] + pv[:HEAD_DIM]
            new_l.append(alpha * l[a] + pv[HEAD_DIM:HEAD_DIM + 1])
        return new_l

    def step(state, cons, prod):
        m8, m_prev, l = state
        m_row = [jnp.max(m8[a], axis=0, keepdims=True) for a in heads]
        l = consume(*cons, m_row, m_prev, l)
        if prod is not None:
            m8 = [jnp.maximum(m8[a], x) for a, x in zip(heads, produce(*prod))]
        return m8, m_row, l

    for a in heads:
        acc_ref[a] = jnp.zeros((HEAD_DIM, tq), F32)
    first_diag = n_diag * i
    m8 = produce(0, first_diag, 0)
    state = (m8, [jnp.full((1, tq), NEG_INIT, F32) for _ in heads],
             [jnp.zeros((1, tq), F32) for _ in heads])
    for dg in range(1, n_diag):
        state = step(state, ((dg - 1) % 2, first_diag + dg - 1), (dg % 2, first_diag + dg, dg))
    pending = (n_diag - 1) % 2
    assert pending == 1 and n_diag == 2

    def pair(t, carry):
        blk = 2 * t
        prev_blk = jnp.where(t == 0, first_diag + n_diag - 1, blk - 1)
        state = step(tuple(list(x) for x in carry), (1, prev_blk), (0, blk, None))
        state = step(state, (0, blk), (1, blk + 1, None))
        return tuple(tuple(x) for x in state)

    state = lax.fori_loop(0, i * n_diag // 2, pair, tuple(tuple(x) for x in state))
    last_blk = jnp.where(i == 0, first_diag + n_diag - 1, first_diag - 1)
    _, _, l = step(tuple(list(x) for x in state), (1, last_blk), None)
    out_t = jnp.concatenate([acc_ref[a] / l[a] for a in heads], axis=0)
    o_ref[0] = out_t.T.astype(o_ref.dtype)


def _attn_prompt(qt, kb, kf, vt, tq):
    b, aw, t = qt.shape
    tk = tq // 2
    gw = ATTN_HEAD_GROUP * HEAD_DIM
    return pl.pallas_call(
        functools.partial(_attn_prompt_kernel, tq=tq, tk=tk),
        scratch_shapes=[pltpu.VMEM((2, ATTN_HEAD_GROUP, tk, tq), F32),
                        pltpu.VMEM((ATTN_HEAD_GROUP, HEAD_DIM, tq), F32)],
        grid=(b, aw // gw, t // tq),
        in_specs=[pl.BlockSpec((1, gw, tq), lambda bi, g, i: (bi, g, i)),
                  pl.BlockSpec((1, t, gw), lambda bi, g, i: (bi, 0, g)),
                  pl.BlockSpec((1, t, LANES), lambda bi, g, i: (bi, 0, 0)),
                  pl.BlockSpec((1, gw, t), lambda bi, g, i: (bi, g, 0))],
        out_specs=pl.BlockSpec((1, tq, gw), lambda bi, g, i: (bi, i, g)),
        out_shape=jax.ShapeDtypeStruct((b, t, aw), BF16),
        compiler_params=_params("parallel", "parallel", "parallel"),
        name="attn_prompt",
    )(qt, kb, kf, vt)


def _attn_cached_kernel(q_ref, ckt_ref, cvt_ref, kn_ref, vn_ref, fp_ref, fn_ref, o_ref,
                        m_ref, l_ref, acc_ref):
    j = pl.program_id(1)
    n_chunks = pl.num_programs(1) - 1
    n_heads = ckt_ref.shape[1]

    @pl.when(j == 0)
    def _():
        m_ref[...] = jnp.full(m_ref.shape, NEG_INIT, F32)
        l_ref[...] = jnp.zeros(l_ref.shape, F32)
        acc_ref[...] = jnp.zeros(acc_ref.shape, F32)

    def update(h, s, pv_of):
        m_old = m_ref[h]
        m_new = jnp.maximum(m_old, jnp.max(s, axis=-1, keepdims=True))
        alpha = jnp.exp2(m_old - m_new)
        p = jnp.exp2(s - m_new)
        l_ref[h] = alpha * l_ref[h] + jnp.sum(p, axis=-1, keepdims=True)
        acc_ref[h] = alpha * acc_ref[h] + pv_of(p.astype(BF16))
        m_ref[h] = m_new

    def q_head(h):
        return q_ref[0, :, h * HEAD_DIM:(h + 1) * HEAD_DIM]

    @pl.when(j < n_chunks)
    def _():
        for h in range(n_heads):
            s = jnp.dot(q_head(h), ckt_ref[0, h].astype(BF16), preferred_element_type=F32)
            s = s - LOG2E * fp_ref[0, h:h + 1, :]
            update(h, s, lambda p, h=h: _scores(p, cvt_ref[0, h].astype(BF16)))

    @pl.when(j == n_chunks)
    def _():
        outs = []
        for h in range(n_heads):
            lanes = slice(h * HEAD_DIM, (h + 1) * HEAD_DIM)
            s = _scores(q_head(h), kn_ref[0, :, lanes]) - LOG2E * fn_ref[0, h:h + 1, :]
            qi = lax.broadcasted_iota(jnp.int32, s.shape, 0)
            ki = lax.broadcasted_iota(jnp.int32, s.shape, 1)
            s = jnp.where(ki <= qi, s, -jnp.inf)
            update(h, s, lambda p, lanes=lanes: jnp.dot(p, vn_ref[0, :, lanes],
                                                        preferred_element_type=F32))
            outs.append(acc_ref[h] / l_ref[h])
        o_ref[0] = jnp.concatenate(outs, axis=1).astype(o_ref.dtype)


def _attn_cached(q, cache_kt, cache_vt, kb, vb, f_past, f_new, chunk):
    b, t, aw = q.shape
    _, nh, dh, p = cache_kt.shape
    n_chunks = p // chunk
    last = n_chunks - 1
    new = pl.BlockSpec((1, t, aw), lambda bi, j: (bi, 0, 0))
    past = pl.BlockSpec((1, nh, dh, chunk), lambda bi, j: (bi, 0, 0, jnp.minimum(j, last)))
    return pl.pallas_call(
        _attn_cached_kernel,
        grid=(b, n_chunks + 1),
        in_specs=[new, past, past, new, new,
                  pl.BlockSpec((1, nh, chunk), lambda bi, j: (bi, 0, jnp.minimum(j, last))),
                  pl.BlockSpec((1, nh, t), lambda bi, j: (bi, 0, 0))],
        out_specs=new,
        out_shape=jax.ShapeDtypeStruct((b, t, aw), BF16),
        scratch_shapes=[pltpu.VMEM((nh, t, 1), F32), pltpu.VMEM((nh, t, 1), F32),
                        pltpu.VMEM((nh, t, dh), F32)],
        compiler_params=_params("parallel", "arbitrary"),
        name="attn_cached",
    )(q, cache_kt, cache_vt, kb, vb, f_past, f_new)


def _mix_kernel(glu_ref, hist_ref, attn_ref, gate_ref, x_ref, g1_ref,
                cw_ref, cb_ref, clg_ref, clb_ref, wap_ref, wcp_ref, wo_ref, l1g_ref, l1b_ref,
                x1_ref, sh_ref, hc_ref, *, alpha, row_chunk):
    nb, tt, cw = glu_ref.shape
    d = x_ref.shape[-1]
    taps = cw_ref.shape[0]
    n_hist = hist_ref.shape[1]
    rows = nb * tt

    ext_ref = sh_ref.at[0]
    length = CONV_HALO + tt

    @pl.when(pl.program_id(1) == 0)
    def _():
        ext_ref[:, 0:SUBLANES, :] = jnp.zeros((nb, SUBLANES, cw), F32)
        ext_ref[:, CONV_HALO - n_hist:CONV_HALO, :] = hist_ref[...]

    ext_ref[:, CONV_HALO:length, :] = glu_ref[...]
    for r in range(1, SUBLANES):
        sh_ref[r, :, 0:length - SUBLANES, :] = ext_ref[:, r:r + length - SUBLANES, :]

    first = CONV_HALO - (taps - 1)
    for b in range(nb):
        for c in range(tt // row_chunk):
            r0 = c * row_chunk
            acc = jnp.zeros((row_chunk, cw), F32)
            for k in range(taps):
                shift = (first + k) % SUBLANES
                base = r0 + first + k - shift
                acc = acc + cw_ref[k:k + 1, :] * sh_ref[shift, b, base:base + row_chunk, :]
            h = _layer_norm(acc + cb_ref[...], clg_ref[...], clb_ref[...])
            h = h * _sigmoid(h)
            hc_ref[b * tt + r0:b * tt + r0 + row_chunk, :] = h.astype(BF16)

    ext_ref[:, 0:CONV_HALO, :] = ext_ref[:, tt:tt + CONV_HALO, :]

    ya = jnp.dot(attn_ref[...].reshape(rows, attn_ref.shape[-1]), wap_ref[...],
                 preferred_element_type=F32)
    yb = jnp.dot(hc_ref[...], wcp_ref[...], preferred_element_type=F32)
    gate = gate_ref[...].reshape(rows, 2 * d)
    merged = gate[:, :d].astype(F32) * ya + gate[:, d:].astype(F32) * yb
    z = jnp.dot(merged.astype(BF16), wo_ref[...], preferred_element_type=F32).reshape(nb, tt, d)
    r = alpha * x_ref[...] + g1_ref[...] * z
    x1_ref[...] = _layer_norm(r, l1g_ref[...], l1b_ref[...])


def _mix(glu, hist, attn, gate, x, g1, wp, alpha, nb, tt):
    b, t, d = x.shape
    cw = glu.shape[-1]
    row_chunk = min(tt, 64)
    row = lambda width: pl.BlockSpec((nb, tt, width), lambda i, j: (i, j, 0))
    per_seq = lambda arr: pl.BlockSpec((nb,) + arr.shape[1:], lambda i, j: (i, 0, 0))
    consts = [wp["conv_w"], wp["conv_b"], wp["conv_ln_g"], wp["conv_ln_b"], wp["w_attn_proj"],
              wp["w_conv_proj"], wp["w_out"], wp["ln1_g"], wp["ln1_b"]]
    return pl.pallas_call(
        functools.partial(_mix_kernel, alpha=alpha, row_chunk=row_chunk),
        grid=(b // nb, t // tt),
        in_specs=[row(cw), per_seq(hist), row(attn.shape[-1]), row(gate.shape[-1]), row(d),
                  per_seq(g1)] + [_const_spec(c.shape) for c in consts],
        out_specs=row(d),
        out_shape=jax.ShapeDtypeStruct((b, t, d), F32),
        scratch_shapes=[pltpu.VMEM((SUBLANES, nb, CONV_HALO + tt, cw), F32),
                        pltpu.VMEM((nb * tt, cw), BF16)],
        compiler_params=_params("parallel", "arbitrary"),
        name="mix",
    )(glu, hist, attn, gate, x, g1, *consts)


def _ffn_kernel(x1_ref, sc_ref, sh_ref, g2_ref, hist_ref, wua_ref, wuv_ref, fw_ref, fb_ref,
                wd_ref, l2g_ref, l2b_ref, y_ref, state_ref, ext_ref, *, alpha):
    nb, tt, d = x1_ref.shape
    dff = wua_ref.shape[1]
    taps = fw_ref.shape[0]
    n_hist = hist_ref.shape[1]
    rows = nb * tt

    @pl.when(pl.program_id(1) == 0)
    def _():
        ext_ref[:, FFN_HALO - n_hist:FFN_HALO, :] = hist_ref[...]

    x1 = x1_ref[...]
    u = x1 * (1.0 + sc_ref[...]) + sh_ref[...]
    ub = u.reshape(rows, d).astype(BF16)
    a2 = jnp.dot(ub, wua_ref[...], preferred_element_type=F32)
    ext_ref[:, FFN_HALO:FFN_HALO + tt, :] = a2.reshape(nb, tt, dff)
    state_ref[...] = ext_ref[:, FFN_HALO + tt - n_hist:FFN_HALO + tt, :]

    first = FFN_HALO - (taps - 1)
    conv = fb_ref[...].reshape(1, 1, dff)
    for k in range(taps):
        conv = conv + fw_ref[k:k + 1, :].reshape(1, 1, dff) * ext_ref[:, first + k:first + k + tt, :]
    v2 = jnp.dot(ub, wuv_ref[...], preferred_element_type=F32)
    conv = conv.reshape(rows, dff)
    h = conv * _sigmoid(conv) * v2

    ext_ref[:, 0:FFN_HALO, :] = ext_ref[:, tt:tt + FFN_HALO, :]

    z = jnp.dot(h.astype(BF16), wd_ref[...], preferred_element_type=F32).reshape(nb, tt, d)
    r = alpha * x1 + g2_ref[...] * z
    y_ref[...] = _layer_norm(r, l2g_ref[...], l2b_ref[...])


def _ffn(x1, sc, sh, g2, hist, wp, alpha, nb, tt):
    b, t, d = x1.shape
    dff = wp["w_up_a"].shape[1]
    n_hist = hist.shape[1]
    row = pl.BlockSpec((nb, tt, d), lambda i, j: (i, j, 0))
    per_seq = lambda arr: pl.BlockSpec((nb,) + arr.shape[1:], lambda i, j: (i, 0, 0))
    consts = [wp["w_up_a"], wp["w_up_v"], wp["ffn_conv_w"], wp["ffn_conv_b"], wp["w_down"],
              wp["ln2_g"], wp["ln2_b"]]
    return pl.pallas_call(
        functools.partial(_ffn_kernel, alpha=alpha),
        grid=(b // nb, t // tt),
        in_specs=[row, per_seq(sc), per_seq(sh), per_seq(g2), per_seq(hist)]
                 + [_const_spec(c.shape) for c in consts],
        out_specs=[row, pl.BlockSpec((nb, n_hist, dff), lambda i, j: (i, 0, 0))],
        out_shape=[jax.ShapeDtypeStruct((b, t, d), F32),
                   jax.ShapeDtypeStruct((b, n_hist, dff), F32)],
        scratch_shapes=[pltpu.VMEM((nb, FFN_HALO + tt, dff), F32)],
        compiler_params=_params("parallel", "arbitrary"),
        name="ffn",
    )(x1, sc, sh, g2, hist, *consts)


def _prepare_weights(w_in, b_f, conv_w, conv_b, conv_ln_g, conv_ln_b, w_attn_proj, w_conv_proj,
                     w_out, ln1_g, ln1_b, w_up, ffn_conv_w, ffn_conv_b, w_down, ln2_g, ln2_b):
    d = w_in.shape[0]
    aw = w_attn_proj.shape[0]
    cw = w_conv_proj.shape[0]
    nh = b_f.shape[0]
    dff = w_down.shape[0]
    s_v, s_f, s_glu = 3 * aw, 3 * aw + nh, 3 * aw + nh + 2 * cw
    row = lambda v: v.reshape(1, -1)
    return dict(
        w_qkv=w_in[:, :s_v].astype(BF16),
        w_f=jnp.pad(w_in[:, s_v:s_f], ((0, 0), (0, LANES - nh))).astype(BF16),
        w_glu=w_in[:, s_f:s_glu].astype(BF16),
        w_gate=w_in[:, s_glu:].astype(BF16),
        b_f=jnp.pad(row(b_f), ((0, 0), (0, LANES - nh))), n_heads=nh,
        conv_w=conv_w, conv_b=row(conv_b), conv_ln_g=row(conv_ln_g), conv_ln_b=row(conv_ln_b),
        w_attn_proj=w_attn_proj.astype(BF16), w_conv_proj=w_conv_proj.astype(BF16),
        w_out=w_out.astype(BF16), ln1_g=row(ln1_g), ln1_b=row(ln1_b),
        w_up_a=w_up[:, :dff].astype(BF16), w_up_v=w_up[:, dff:].astype(BF16),
        ffn_conv_w=ffn_conv_w, ffn_conv_b=row(ffn_conv_b),
        w_down=w_down.astype(BF16), ln2_g=row(ln2_g), ln2_b=row(ln2_b),
    )


def _trunk_layer(x, mod, wp, alpha, past, conv_hist, ffn_hist, nb, tt, tq):
    b, t, d = x.shape
    sh1, sc1, g1, sh2, sc2, g2 = mod
    nh = wp["n_heads"]
    q, k32, v32, kb, vb, logf, glu, gate = _inproj(x, sc1, sh1, wp, nb, tt, transposed=past is None)

    if past is None:
        _, pieces = _cumsum_time(logf)
        kf = jnp.transpose(pieces, (1, 3, 2, 0)).reshape(b, t, -1)
        kf = jnp.pad(kf, ((0, 0), (0, 0), (0, LANES - kf.shape[-1])))
        attn = _attn_prompt(q, kb, kf, vb, tq)
        by_row = lambda a: jnp.transpose(a.reshape(b, nh, -1, t), (0, 3, 1, 2))
        k_out, v_out, logf_out = by_row(k32), by_row(v32), jnp.transpose(logf, (0, 2, 1))
    else:
        cache_k, cache_v, cache_logf = past
        p = cache_k.shape[1]
        f_all, _ = _cumsum_time(jnp.concatenate(
            [jnp.transpose(cache_logf, (0, 2, 1)), jnp.transpose(logf, (0, 2, 1))], axis=2))
        by_feature = lambda a: jnp.transpose(a, (0, 2, 3, 1))
        attn = _attn_cached(q, by_feature(cache_k), by_feature(cache_v), kb, vb,
                            f_all[:, :, :p], f_all[:, :, p:], tq)
        heads = lambda a: a.reshape(b, t, nh, -1)
        k_out, v_out, logf_out = heads(k32), heads(v32), logf

    x1 = _mix(glu, conv_hist, attn, gate, x, g1, wp, alpha, nb, tt)
    y, ffn_state = _ffn(x1, sc2, sh2, g2, ffn_hist, wp, alpha, nb, tt)
    conv_state = glu[:, t - conv_hist.shape[1]:, :]
    return y, k_out, v_out, logf_out, conv_state, ffn_state


def kernel(x_prompt, x_sample, c_prompt, c_sample, cache_k, cache_v, cache_logf, state_conv, state_ffn_conv, w_ada, b_ada, w_in, b_f, conv_w, conv_b, conv_ln_g, conv_ln_b, w_attn_proj, w_conv_proj, w_out, ln1_g, ln1_b, w_up, ffn_conv_w, ffn_conv_b, w_down, ln2_g, ln2_b):
    depth = w_ada.shape[0]
    bp, tp, d = x_prompt.shape
    bs, ts, _ = x_sample.shape
    nh = b_f.shape[1]
    alpha = float((2 * depth) ** 0.25)
    prompt_tile = min(tp, 512)
    prompt_tq = min(tp, 512)

    y_p, y_s = x_prompt, x_sample
    c_all = jnp.concatenate([c_prompt, c_sample], axis=0)
    outs = [[] for _ in range(10)]
    for l in range(depth):
        wp = _prepare_weights(w_in[l], b_f[l], conv_w[l], conv_b[l], conv_ln_g[l], conv_ln_b[l],
                              w_attn_proj[l], w_conv_proj[l], w_out[l], ln1_g[l], ln1_b[l],
                              w_up[l], ffn_conv_w[l], ffn_conv_b[l], w_down[l], ln2_g[l], ln2_b[l])
        mod = _ada(c_all, w_ada[l], b_ada[l])
        mod_p = [mod[:bp, i * d:(i + 1) * d].reshape(bp, 1, d) for i in range(N_MOD)]
        mod_s = [mod[bp:, i * d:(i + 1) * d].reshape(bs, 1, d) for i in range(N_MOD)]

        zc = jnp.zeros((bp,) + state_conv.shape[2:], F32)
        zf = jnp.zeros((bp,) + state_ffn_conv.shape[2:], F32)
        y_p, kp, vp, fp, cp, ffp = _trunk_layer(
            y_p, mod_p, wp, alpha, None, zc, zf, nb=1, tt=prompt_tile, tq=prompt_tq)

        past_len = cache_k.shape[2]
        past = (cache_k[l], cache_v[l], cache_logf[l])
        y_s, ks, vs, fs, cs, ffs = _trunk_layer(
            y_s, mod_s, wp, alpha, past, state_conv[l], state_ffn_conv[l],
            nb=bs, tt=ts, tq=min(past_len, 2048))

        for lst, val in zip(outs, (kp, vp, fp, cp, ffp, ks, vs, fs, cs, ffs)):
            lst.append(val)
    stacked = [jnp.stack(lst) for lst in outs]
    return (y_p, y_s, *stacked)
```

```python
import functools

import jax
import jax.numpy as jnp
from jax import lax
from jax.experimental import pallas as pl
from jax.experimental.pallas import tpu as pltpu

F32 = jnp.float32
BF16 = jnp.bfloat16

LN_EPS = 1e-5
N_MOD = 6
HEAD_DIM = 64
LANES = 128
SUBLANES = 8
HEADS_PER_BLOCK = LANES // HEAD_DIM
ATTN_HEAD_GROUP = 8
CONV_HALO = 32
FFN_HALO = 8
VMEM_LIMIT = 56 * 1024 * 1024
NEG_INIT = -1e30
LOG2E = 1.4426950408889634
N_SPLIT = 3
ONES_ROWS = 16


def _sigmoid(x):
    return 0.5 * (jnp.tanh(0.5 * x) + 1.0)


def _log_sigmoid(x):
    return jnp.minimum(x, 0.0) - jnp.log1p(jnp.exp(-jnp.abs(x)))


def _layer_norm(x, g, b):
    mu = jnp.mean(x, axis=-1, keepdims=True)
    xc = x - mu
    var = jnp.mean(xc * xc, axis=-1, keepdims=True)
    return xc * lax.rsqrt(var + LN_EPS) * g + b


def _const_spec(shape):
    zeros = (0,) * len(shape)
    return pl.BlockSpec(shape, lambda *_: zeros)


def _params(*semantics):
    return pltpu.CompilerParams(dimension_semantics=semantics, vmem_limit_bytes=VMEM_LIMIT)


def _ada_kernel(c_ref, w_ref, b_ref, o_ref):
    o_ref[...] = jnp.dot(c_ref[...], w_ref[...], preferred_element_type=F32) + b_ref[...]


def _ada(c_all, w_ada, b_ada):
    rows, d = c_all.shape
    n = w_ada.shape[1]
    return pl.pallas_call(
        _ada_kernel,
        grid=(n // d,),
        in_specs=[pl.BlockSpec((rows, d), lambda j: (0, 0)),
                  pl.BlockSpec((d, d), lambda j: (0, j)),
                  pl.BlockSpec((1, d), lambda j: (0, j))],
        out_specs=pl.BlockSpec((rows, d), lambda j: (0, j)),
        out_shape=jax.ShapeDtypeStruct((rows, n), F32),
        compiler_params=_params("arbitrary"),
        name="ada",
    )(c_all, w_ada, b_ada.reshape(1, n))


def _inproj_kernel(x_ref, sc_ref, sh_ref, wqkv_ref, wf_ref, wglu_ref, wgate_ref, bf_ref,
                   q_ref, k32_ref, v32_ref, kb_ref, vb_ref, logf_ref, glu_ref, gate_ref,
                   *, transposed):
    nb, tt, d = x_ref.shape
    aw = kb_ref.shape[-1]
    cw = glu_ref.shape[-1]
    rows = nb * tt
    u = x_ref[...] * (1.0 + sc_ref[...]) + sh_ref[...]
    ub = u.reshape(rows, d).astype(BF16)

    def proj(w_ref, lo, hi):
        return jnp.dot(ub, w_ref[:, lo:hi], preferred_element_type=F32)

    q = proj(wqkv_ref, 0, aw) * (HEAD_DIM ** -0.5 * LOG2E)
    k = proj(wqkv_ref, aw, 2 * aw)
    kb_ref[...] = k.reshape(nb, tt, aw).astype(BF16)
    v = proj(wqkv_ref, 2 * aw, 3 * aw)
    logf = _log_sigmoid(jnp.dot(ub, wf_ref[...], preferred_element_type=F32) + bf_ref[...])
    if transposed:
        vt = v.T
        q_ref[0] = q.T.astype(BF16)
        k32_ref[0] = k.T
        v32_ref[0] = vt
        vb_ref[0] = vt.astype(BF16)
        logf_ref[0] = logf.T[:logf_ref.shape[1]]
    else:
        q_ref[...] = q.reshape(nb, tt, aw).astype(BF16)
        k32_ref[...] = k.reshape(nb, tt, aw)
        v32_ref[...] = v.reshape(nb, tt, aw)
        vb_ref[...] = v.reshape(nb, tt, aw).astype(BF16)
        nh = logf_ref.shape[-1]
        logf_ref[...] = logf[:, :nh].reshape(nb, tt, nh)

    glu = proj(wglu_ref, 0, cw) * _sigmoid(proj(wglu_ref, cw, 2 * cw))
    glu_ref[...] = glu.reshape(nb, tt, cw)

    gate = _sigmoid(jnp.dot(ub, wgate_ref[...], preferred_element_type=F32))
    gate_ref[...] = gate.reshape(nb, tt, gate.shape[-1]).astype(BF16)


def _inproj(x, sc, sh, wp, nb, tt, transposed):
    b, t, d = x.shape
    aw = wp["w_qkv"].shape[1] // 3
    cw = wp["w_glu"].shape[1] // 2
    gw = wp["w_gate"].shape[1]
    nh = wp["n_heads"]
    assert not transposed or nb == 1
    row = lambda width: pl.BlockSpec((nb, tt, width), lambda i, j: (i, j, 0))
    mod = pl.BlockSpec((nb, 1, d), lambda i, j: (i, 0, 0))
    sds = lambda width, dt: jax.ShapeDtypeStruct((b, t, width), dt)
    if transposed:
        col = lambda width: pl.BlockSpec((1, width, tt), lambda i, j: (i, 0, j))
        sds_t = lambda width, dt: jax.ShapeDtypeStruct((b, width, t), dt)
    else:
        col, sds_t = row, sds
    return pl.pallas_call(
        functools.partial(_inproj_kernel, transposed=transposed),
        grid=(b // nb, t // tt),
        in_specs=[row(d), mod, mod,
                  _const_spec(wp["w_qkv"].shape), _const_spec(wp["w_f"].shape),
                  _const_spec(wp["w_glu"].shape), _const_spec(wp["w_gate"].shape),
                  _const_spec(wp["b_f"].shape)],
        out_specs=[col(aw), col(aw), col(aw), row(aw), col(aw), col(nh), row(cw), row(gw)],
        out_shape=[sds_t(aw, BF16), sds_t(aw, F32), sds_t(aw, F32), sds(aw, BF16), sds_t(aw, BF16),
                   sds_t(nh, F32), sds(cw, F32), sds(gw, BF16)],
        compiler_params=_params("parallel", "parallel"),
        name="inproj",
    )(x, sc, sh, wp["w_qkv"], wp["w_f"], wp["w_glu"], wp["w_gate"], wp["b_f"])


def _split3(x):
    hi = x.astype(BF16)
    r1 = x - hi.astype(F32)
    mid = r1.astype(BF16)
    lo = (r1 - mid.astype(F32)).astype(BF16)
    return hi, mid, lo


def _cumsum_kernel(x_ref, o_ref, piece_ref):
    gb, r, _ = x_ref.shape
    n = gb * r
    li = lax.broadcasted_iota(jnp.int32, (LANES, LANES), 0)
    lj = lax.broadcasted_iota(jnp.int32, (LANES, LANES), 1)
    upper = jnp.where(li <= lj, 1.0, 0.0).astype(BF16)
    ri = lax.broadcasted_iota(jnp.int32, (n, n), 0)
    rj = lax.broadcasted_iota(jnp.int32, (n, n), 1)
    earlier = jnp.where((rj < ri) & (rj // r == ri // r), 1.0, 0.0).astype(BF16)
    x = x_ref[...].reshape(n, LANES)
    within = sum(jnp.dot(p, upper, preferred_element_type=F32) for p in _split3(x))
    total = jnp.broadcast_to(within[:, LANES - 1:LANES], (n, LANES))
    carry = sum(jnp.dot(earlier, p, preferred_element_type=F32) for p in _split3(total))
    f = within + carry
    o_ref[...] = f.reshape(gb, r, LANES)
    for p, piece in enumerate(_split3(f * (-LOG2E))):
        piece_ref[p] = piece.reshape(gb, r, LANES)


def _cumsum_time(logf_bht):
    b, h, l = logf_bht.shape
    tile = 2 * SUBLANES * LANES
    lp = -(-l // tile) * tile
    x = jnp.pad(logf_bht, ((0, 0), (0, 0), (0, lp - l))) if lp != l else logf_bht
    g, r = b * h, lp // LANES
    gb = SUBLANES
    f, pieces = pl.pallas_call(
        _cumsum_kernel,
        grid=(g // gb,),
        in_specs=[pl.BlockSpec((gb, r, LANES), lambda i: (i, 0, 0))],
        out_specs=[pl.BlockSpec((gb, r, LANES), lambda i: (i, 0, 0)),
                   pl.BlockSpec((N_SPLIT, gb, r, LANES), lambda i: (0, i, 0, 0))],
        out_shape=[jax.ShapeDtypeStruct((g, r, LANES), F32),
                   jax.ShapeDtypeStruct((N_SPLIT, g, r, LANES), BF16)],
        compiler_params=_params("parallel"),
        name="cumsum",
    )(x.reshape(g, r, LANES))
    return f.reshape(b, h, lp)[:, :, :l], pieces.reshape(N_SPLIT, b, h, lp)[:, :, :, :l]


def _scores(qa, k):
    return lax.dot_general(qa, k, (((1,), (1,)), ((), ())), preferred_element_type=F32)


def _attn_prompt_kernel(qt_ref, k_ref, kf_ref, vt_ref, o_ref, s_ref, acc_ref, *, tq, tk):
    group = pl.program_id(1)
    i = pl.program_id(2)
    n_heads = qt_ref.shape[1] // HEAD_DIM
    heads = range(n_heads)
    zeros = jnp.zeros((HEAD_DIM, tq), BF16)
    row = lax.broadcasted_iota(jnp.int32, (LANES, tq), 0)
    q_aug = []
    for a in heads:
        first = N_SPLIT * (n_heads * group + a)
        pick = jnp.where((row >= first) & (row < first + N_SPLIT), 1.0, 0.0).astype(BF16)
        pair = a // HEADS_PER_BLOCK
        parts = [qt_ref[0, h * HEAD_DIM:(h + 1) * HEAD_DIM, :] if h == a else zeros
                 for h in range(pair * HEADS_PER_BLOCK, (pair + 1) * HEADS_PER_BLOCK)]
        q_aug.append(jnp.concatenate(parts + [pick], axis=0))

    n_diag = tq // tk

    def produce(slot, blk, diag):
        start = pl.multiple_of(blk * tk, tk)
        kf = kf_ref[0, pl.ds(start, tk), :]
        k_aug = [jnp.concatenate([k_ref[0, pl.ds(start, tk), p * LANES:(p + 1) * LANES], kf], axis=1)
                 for p in range(n_heads // HEADS_PER_BLOCK)]
        hidden = 0 if diag is None else diag * tk
        maxima = []
        for a in heads:
            st = jnp.dot(k_aug[a // HEADS_PER_BLOCK], q_aug[a][:, hidden:],
                         preferred_element_type=F32)
            if diag is not None:
                ki = lax.broadcasted_iota(jnp.int32, st.shape, 0)
                qi = lax.broadcasted_iota(jnp.int32, st.shape, 1)
                st = jnp.where(ki <= qi, st, -jnp.inf)
                if hidden:
                    st = jnp.concatenate([jnp.full((tk, hidden), -jnp.inf, F32), st], axis=1)
            s_ref[slot, a] = st
            maxima.append(jnp.max(st.reshape(tk // SUBLANES, SUBLANES, tq), axis=0))
        return maxima

    ones = jnp.ones((ONES_ROWS, tk), BF16)

    def consume(slot, blk, m_row, m_prev, l):
        start = pl.multiple_of(blk * tk, tk)
        new_l = []
        for a in heads:
            alpha = jnp.exp2(m_prev[a] - m_row[a])
            pt = jnp.exp2(s_ref[slot, a] - m_row[a])
            vt = vt_ref[0, a * HEAD_DIM:(a + 1) * HEAD_DIM, pl.ds(start, tk)]
            pv = jnp.dot(jnp.concatenate([vt, ones], axis=0), pt.astype(BF16),
                         preferred_element_type=F32)
            acc_ref[a] = alpha * acc_ref[a] + pv[:HEAD_DIM]
            new_l.append(alpha * l[a] + pv[HEAD_DIM:HEAD_DIM + 1])
        return new_l

    def step(state, cons, prod):
        m8, m_prev, l = state
        m_row = [jnp.max(m8[a], axis=0, keepdims=True) for a in heads]
        l = consume(*cons, m_row, m_prev, l)
        if prod is not None:
            m8 = [jnp.maximum(m8[a], x) for a, x in zip(heads, produce(*prod))]
        return m8, m_row, l

    for a in heads:
        acc_ref[a] = jnp.zeros((HEAD_DIM, tq), F32)
    first_diag = n_diag * i
    m8 = produce(0, first_diag, 0)
    state = (m8, [jnp.full((1, tq), NEG_INIT, F32) for _ in heads],
             [jnp.zeros((1, tq), F32) for _ in heads])
    for dg in range(1, n_diag):
        state = step(state, ((dg - 1) % 2, first_diag + dg - 1), (dg % 2, first_diag + dg, dg))
    pending = (n_diag - 1) % 2
    assert pending == 1 and n_diag == 2

    def pair(t, carry):
        blk = 2 * t
        prev_blk = jnp.where(t == 0, first_diag + n_diag - 1, blk - 1)
        state = step(tuple(list(x) for x in carry), (1, prev_blk), (0, blk, None))
        state = step(state, (0, blk), (1, blk + 1, None))
        return tuple(tuple(x) for x in state)

    n_pairs = i * n_diag // 2
    state = lax.fori_loop(0, n_pairs // 2, lambda u, c: pair(2 * u + 1, pair(2 * u, c)),
                          tuple(tuple(x) for x in state))
    state = lax.cond(n_pairs % 2 == 1, lambda c: pair(n_pairs - 1, c), lambda c: c, state)
    last_blk = jnp.where(i == 0, first_diag + n_diag - 1, first_diag - 1)
    _, _, l = step(tuple(list(x) for x in state), (1, last_blk), None)
    out_t = jnp.concatenate([acc_ref[a] / l[a] for a in heads], axis=0)
    o_ref[0] = out_t.T.astype(o_ref.dtype)


def _attn_prompt(qt, kb, kf, vt, tq):
    b, aw, t = qt.shape
    tk = tq // 2
    gw = ATTN_HEAD_GROUP * HEAD_DIM
    return pl.pallas_call(
        functools.partial(_attn_prompt_kernel, tq=tq, tk=tk),
        scratch_shapes=[pltpu.VMEM((2, ATTN_HEAD_GROUP, tk, tq), F32),
                        pltpu.VMEM((ATTN_HEAD_GROUP, HEAD_DIM, tq), F32)],
        grid=(b, aw // gw, t // tq),
        in_specs=[pl.BlockSpec((1, gw, tq), lambda bi, g, i: (bi, g, i)),
                  pl.BlockSpec((1, t, gw), lambda bi, g, i: (bi, 0, g)),
                  pl.BlockSpec((1, t, LANES), lambda bi, g, i: (bi, 0, 0)),
                  pl.BlockSpec((1, gw, t), lambda bi, g, i: (bi, g, 0))],
        out_specs=pl.BlockSpec((1, tq, gw), lambda bi, g, i: (bi, i, g)),
        out_shape=jax.ShapeDtypeStruct((b, t, aw), BF16),
        compiler_params=_params("parallel", "parallel", "parallel"),
        name="attn_prompt",
    )(qt, kb, kf, vt)


def _attn_cached_kernel(q_ref, ckt_ref, cvt_ref, kn_ref, vn_ref, fp_ref, fn_ref, o_ref,
                        m_ref, l_ref, acc_ref):
    j = pl.program_id(1)
    n_chunks = pl.num_programs(1) - 1
    n_heads = ckt_ref.shape[1]

    @pl.when(j == 0)
    def _():
        m_ref[...] = jnp.full(m_ref.shape, NEG_INIT, F32)
        l_ref[...] = jnp.zeros(l_ref.shape, F32)
        acc_ref[...] = jnp.zeros(acc_ref.shape, F32)

    def update(h, s, pv_of):
        m_old = m_ref[h]
        m_new = jnp.maximum(m_old, jnp.max(s, axis=-1, keepdims=True))
        alpha = jnp.exp2(m_old - m_new)
        p = jnp.exp2(s - m_new)
        l_ref[h] = alpha * l_ref[h] + jnp.sum(p, axis=-1, keepdims=True)
        acc_ref[h] = alpha * acc_ref[h] + pv_of(p.astype(BF16))
        m_ref[h] = m_new

    def q_head(h):
        return q_ref[0, :, h * HEAD_DIM:(h + 1) * HEAD_DIM]

    @pl.when(j < n_chunks)
    def _():
        for h in range(n_heads):
            s = jnp.dot(q_head(h), ckt_ref[0, h].astype(BF16), preferred_element_type=F32)
            s = s - LOG2E * fp_ref[0, h:h + 1, :]
            update(h, s, lambda p, h=h: _scores(p, cvt_ref[0, h].astype(BF16)))

    @pl.when(j == n_chunks)
    def _():
        outs = []
        for h in range(n_heads):
            lanes = slice(h * HEAD_DIM, (h + 1) * HEAD_DIM)
            s = _scores(q_head(h), kn_ref[0, :, lanes]) - LOG2E * fn_ref[0, h:h + 1, :]
            qi = lax.broadcasted_iota(jnp.int32, s.shape, 0)
            ki = lax.broadcasted_iota(jnp.int32, s.shape, 1)
            s = jnp.where(ki <= qi, s, -jnp.inf)
            update(h, s, lambda p, lanes=lanes: jnp.dot(p, vn_ref[0, :, lanes],
                                                        preferred_element_type=F32))
            outs.append(acc_ref[h] / l_ref[h])
        o_ref[0] = jnp.concatenate(outs, axis=1).astype(o_ref.dtype)


def _attn_cached(q, cache_kt, cache_vt, kb, vb, f_past, f_new, chunk):
    b, t, aw = q.shape
    _, nh, dh, p = cache_kt.shape
    n_chunks = p // chunk
    last = n_chunks - 1
    new = pl.BlockSpec((1, t, aw), lambda bi, j: (bi, 0, 0))
    past = pl.BlockSpec((1, nh, dh, chunk), lambda bi, j: (bi, 0, 0, jnp.minimum(j, last)))
    return pl.pallas_call(
        _attn_cached_kernel,
        grid=(b, n_chunks + 1),
        in_specs=[new, past, past, new, new,
                  pl.BlockSpec((1, nh, chunk), lambda bi, j: (bi, 0, jnp.minimum(j, last))),
                  pl.BlockSpec((1, nh, t), lambda bi, j: (bi, 0, 0))],
        out_specs=new,
        out_shape=jax.ShapeDtypeStruct((b, t, aw), BF16),
        scratch_shapes=[pltpu.VMEM((nh, t, 1), F32), pltpu.VMEM((nh, t, 1), F32),
                        pltpu.VMEM((nh, t, dh), F32)],
        compiler_params=_params("parallel", "arbitrary"),
        name="attn_cached",
    )(q, cache_kt, cache_vt, kb, vb, f_past, f_new)


def _mix_kernel(glu_ref, hist_ref, attn_ref, gate_ref, x_ref, g1_ref,
                cw_ref, cb_ref, clg_ref, clb_ref, wap_ref, wcp_ref, wo_ref, l1g_ref, l1b_ref,
                x1_ref, sh_ref, hc_ref, *, alpha, row_chunk):
    nb, tt, cw = glu_ref.shape
    d = x_ref.shape[-1]
    taps = cw_ref.shape[0]
    n_hist = hist_ref.shape[1]
    rows = nb * tt

    ext_ref = sh_ref.at[0]
    length = CONV_HALO + tt

    @pl.when(pl.program_id(1) == 0)
    def _():
        ext_ref[:, 0:SUBLANES, :] = jnp.zeros((nb, SUBLANES, cw), F32)
        ext_ref[:, CONV_HALO - n_hist:CONV_HALO, :] = hist_ref[...]

    ext_ref[:, CONV_HALO:length, :] = glu_ref[...]
    for r in range(1, SUBLANES):
        sh_ref[r, :, 0:length - SUBLANES, :] = ext_ref[:, r:r + length - SUBLANES, :]

    first = CONV_HALO - (taps - 1)
    for b in range(nb):
        for c in range(tt // row_chunk):
            r0 = c * row_chunk
            acc = jnp.zeros((row_chunk // SUBLANES, SUBLANES, cw), F32)
            for k in range(taps):
                shift = (first + k) % SUBLANES
                base = r0 + first + k - shift
                rows_k = sh_ref[shift, b, base:base + row_chunk, :]
                acc = acc + cw_ref[k] * rows_k.reshape(row_chunk // SUBLANES, SUBLANES, cw)
            h = _layer_norm(acc.reshape(row_chunk, cw) + cb_ref[...], clg_ref[...], clb_ref[...])
            h = h * _sigmoid(h)
            hc_ref[b * tt + r0:b * tt + r0 + row_chunk, :] = h.astype(BF16)

    ext_ref[:, 0:CONV_HALO, :] = ext_ref[:, tt:tt + CONV_HALO, :]

    ya = jnp.dot(attn_ref[...].reshape(rows, attn_ref.shape[-1]), wap_ref[...],
                 preferred_element_type=F32)
    yb = jnp.dot(hc_ref[...], wcp_ref[...], preferred_element_type=F32)
    gate = gate_ref[...].reshape(rows, 2 * d)
    merged = gate[:, :d].astype(F32) * ya + gate[:, d:].astype(F32) * yb
    z = jnp.dot(merged.astype(BF16), wo_ref[...], preferred_element_type=F32).reshape(nb, tt, d)
    r = alpha * x_ref[...] + g1_ref[...] * z
    x1_ref[...] = _layer_norm(r, l1g_ref[...], l1b_ref[...])


def _mix(glu, hist, attn, gate, x, g1, wp, alpha, nb, tt):
    b, t, d = x.shape
    cw = glu.shape[-1]
    row_chunk = min(tt, 64)
    row = lambda width: pl.BlockSpec((nb, tt, width), lambda i, j: (i, j, 0))
    per_seq = lambda arr: pl.BlockSpec((nb,) + arr.shape[1:], lambda i, j: (i, 0, 0))
    consts = [wp["conv_w"], wp["conv_b"], wp["conv_ln_g"], wp["conv_ln_b"], wp["w_attn_proj"],
              wp["w_conv_proj"], wp["w_out"], wp["ln1_g"], wp["ln1_b"]]
    return pl.pallas_call(
        functools.partial(_mix_kernel, alpha=alpha, row_chunk=row_chunk),
        grid=(b // nb, t // tt),
        in_specs=[row(cw), per_seq(hist), row(attn.shape[-1]), row(gate.shape[-1]), row(d),
                  per_seq(g1)] + [_const_spec(c.shape) for c in consts],
        out_specs=row(d),
        out_shape=jax.ShapeDtypeStruct((b, t, d), F32),
        scratch_shapes=[pltpu.VMEM((SUBLANES, nb, CONV_HALO + tt, cw), F32),
                        pltpu.VMEM((nb * tt, cw), BF16)],
        compiler_params=_params("parallel", "arbitrary"),
        name="mix",
    )(glu, hist, attn, gate, x, g1, *consts)


def _ffn_kernel(x1_ref, sc_ref, sh_ref, g2_ref, hist_ref, wua_ref, wuv_ref, fw_ref, fb_ref,
                wd_ref, l2g_ref, l2b_ref, y_ref, state_ref, ext_ref, *, alpha):
    nb, tt, d = x1_ref.shape
    dff = wua_ref.shape[1]
    taps = fw_ref.shape[0]
    n_hist = hist_ref.shape[1]
    rows = nb * tt

    @pl.when(pl.program_id(1) == 0)
    def _():
        ext_ref[:, FFN_HALO - n_hist:FFN_HALO, :] = hist_ref[...]

    x1 = x1_ref[...]
    u = x1 * (1.0 + sc_ref[...]) + sh_ref[...]
    ub = u.reshape(rows, d).astype(BF16)
    a2 = jnp.dot(ub, wua_ref[...], preferred_element_type=F32)
    ext_ref[:, FFN_HALO:FFN_HALO + tt, :] = a2.reshape(nb, tt, dff)
    state_ref[...] = ext_ref[:, FFN_HALO + tt - n_hist:FFN_HALO + tt, :]

    first = FFN_HALO - (taps - 1)
    conv = fb_ref[...].reshape(1, 1, dff)
    for k in range(taps):
        conv = conv + fw_ref[k:k + 1, :].reshape(1, 1, dff) * ext_ref[:, first + k:first + k + tt, :]
    v2 = jnp.dot(ub, wuv_ref[...], preferred_element_type=F32)
    conv = conv.reshape(rows, dff)
    h = conv * _sigmoid(conv) * v2

    ext_ref[:, 0:FFN_HALO, :] = ext_ref[:, tt:tt + FFN_HALO, :]

    z = jnp.dot(h.astype(BF16), wd_ref[...], preferred_element_type=F32).reshape(nb, tt, d)
    r = alpha * x1 + g2_ref[...] * z
    y_ref[...] = _layer_norm(r, l2g_ref[...], l2b_ref[...])


def _ffn(x1, sc, sh, g2, hist, wp, alpha, nb, tt):
    b, t, d = x1.shape
    dff = wp["w_up_a"].shape[1]
    n_hist = hist.shape[1]
    row = pl.BlockSpec((nb, tt, d), lambda i, j: (i, j, 0))
    per_seq = lambda arr: pl.BlockSpec((nb,) + arr.shape[1:], lambda i, j: (i, 0, 0))
    consts = [wp["w_up_a"], wp["w_up_v"], wp["ffn_conv_w"], wp["ffn_conv_b"], wp["w_down"],
              wp["ln2_g"], wp["ln2_b"]]
    return pl.pallas_call(
        functools.partial(_ffn_kernel, alpha=alpha),
        grid=(b // nb, t // tt),
        in_specs=[row, per_seq(sc), per_seq(sh), per_seq(g2), per_seq(hist)]
                 + [_const_spec(c.shape) for c in consts],
        out_specs=[row, pl.BlockSpec((nb, n_hist, dff), lambda i, j: (i, 0, 0))],
        out_shape=[jax.ShapeDtypeStruct((b, t, d), F32),
                   jax.ShapeDtypeStruct((b, n_hist, dff), F32)],
        scratch_shapes=[pltpu.VMEM((nb, FFN_HALO + tt, dff), F32)],
        compiler_params=_params("parallel", "arbitrary"),
        name="ffn",
    )(x1, sc, sh, g2, hist, *consts)


def _prepare_weights(w_in, b_f, conv_w, conv_b, conv_ln_g, conv_ln_b, w_attn_proj, w_conv_proj,
                     w_out, ln1_g, ln1_b, w_up, ffn_conv_w, ffn_conv_b, w_down, ln2_g, ln2_b):
    d = w_in.shape[0]
    aw = w_attn_proj.shape[0]
    cw = w_conv_proj.shape[0]
    nh = b_f.shape[0]
    dff = w_down.shape[0]
    s_v, s_f, s_glu = 3 * aw, 3 * aw + nh, 3 * aw + nh + 2 * cw
    row = lambda v: v.reshape(1, -1)
    return dict(
        w_qkv=w_in[:, :s_v].astype(BF16),
        w_f=jnp.pad(w_in[:, s_v:s_f], ((0, 0), (0, LANES - nh))).astype(BF16),
        w_glu=w_in[:, s_f:s_glu].astype(BF16),
        w_gate=w_in[:, s_glu:].astype(BF16),
        b_f=jnp.pad(row(b_f), ((0, 0), (0, LANES - nh))), n_heads=nh,
        conv_w=jnp.broadcast_to(conv_w[:, None, :], (conv_w.shape[0], SUBLANES, cw)),
        conv_b=row(conv_b), conv_ln_g=row(conv_ln_g), conv_ln_b=row(conv_ln_b),
        w_attn_proj=w_attn_proj.astype(BF16), w_conv_proj=w_conv_proj.astype(BF16),
        w_out=w_out.astype(BF16), ln1_g=row(ln1_g), ln1_b=row(ln1_b),
        w_up_a=w_up[:, :dff].astype(BF16), w_up_v=w_up[:, dff:].astype(BF16),
        ffn_conv_w=ffn_conv_w, ffn_conv_b=row(ffn_conv_b),
        w_down=w_down.astype(BF16), ln2_g=row(ln2_g), ln2_b=row(ln2_b),
    )


def _trunk_layer(x, mod, wp, alpha, past, conv_hist, ffn_hist, nb, tt, tq):
    b, t, d = x.shape
    sh1, sc1, g1, sh2, sc2, g2 = mod
    nh = wp["n_heads"]
    q, k32, v32, kb, vb, logf, glu, gate = _inproj(x, sc1, sh1, wp, nb, tt, transposed=past is None)

    if past is None:
        _, pieces = _cumsum_time(logf)
        kf = jnp.transpose(pieces, (1, 3, 2, 0)).reshape(b, t, -1)
        kf = jnp.pad(kf, ((0, 0), (0, 0), (0, LANES - kf.shape[-1])))
        attn = _attn_prompt(q, kb, kf, vb, tq)
        by_row = lambda a: jnp.transpose(a.reshape(b, nh, -1, t), (0, 3, 1, 2))
        k_out, v_out, logf_out = by_row(k32), by_row(v32), jnp.transpose(logf, (0, 2, 1))
    else:
        cache_k, cache_v, cache_logf = past
        p = cache_k.shape[1]
        f_all, _ = _cumsum_time(jnp.concatenate(
            [jnp.transpose(cache_logf, (0, 2, 1)), jnp.transpose(logf, (0, 2, 1))], axis=2))
        by_feature = lambda a: jnp.transpose(a, (0, 2, 3, 1))
        attn = _attn_cached(q, by_feature(cache_k), by_feature(cache_v), kb, vb,
                            f_all[:, :, :p], f_all[:, :, p:], tq)
        heads = lambda a: a.reshape(b, t, nh, -1)
        k_out, v_out, logf_out = heads(k32), heads(v32), logf

    x1 = _mix(glu, conv_hist, attn, gate, x, g1, wp, alpha, nb, tt)
    y, ffn_state = _ffn(x1, sc2, sh2, g2, ffn_hist, wp, alpha, nb, tt)
    conv_state = glu[:, t - conv_hist.shape[1]:, :]
    return y, k_out, v_out, logf_out, conv_state, ffn_state


def kernel(x_prompt, x_sample, c_prompt, c_sample, cache_k, cache_v, cache_logf, state_conv, state_ffn_conv, w_ada, b_ada, w_in, b_f, conv_w, conv_b, conv_ln_g, conv_ln_b, w_attn_proj, w_conv_proj, w_out, ln1_g, ln1_b, w_up, ffn_conv_w, ffn_conv_b, w_down, ln2_g, ln2_b):
    depth = w_ada.shape[0]
    bp, tp, d = x_prompt.shape
    bs, ts, _ = x_sample.shape
    nh = b_f.shape[1]
    alpha = float((2 * depth) ** 0.25)
    prompt_tile = min(tp, 512)
    prompt_tq = min(tp, 512)

    y_p, y_s = x_prompt, x_sample
    c_all = jnp.concatenate([c_prompt, c_sample], axis=0)
    outs = [[] for _ in range(10)]
    for l in range(depth):
        wp = _prepare_weights(w_in[l], b_f[l], conv_w[l], conv_b[l], conv_ln_g[l], conv_ln_b[l],
                              w_attn_proj[l], w_conv_proj[l], w_out[l], ln1_g[l], ln1_b[l],
                              w_up[l], ffn_conv_w[l], ffn_conv_b[l], w_down[l], ln2_g[l], ln2_b[l])
        mod = _ada(c_all, w_ada[l], b_ada[l])
        mod_p = [mod[:bp, i * d:(i + 1) * d].reshape(bp, 1, d) for i in range(N_MOD)]
        mod_s = [mod[bp:, i * d:(i + 1) * d].reshape(bs, 1, d) for i in range(N_MOD)]

        zc = jnp.zeros((bp,) + state_conv.shape[2:], F32)
        zf = jnp.zeros((bp,) + state_ffn_conv.shape[2:], F32)
        y_p, kp, vp, fp, cp, ffp = _trunk_layer(
            y_p, mod_p, wp, alpha, None, zc, zf, nb=1, tt=prompt_tile, tq=prompt_tq)

        past_len = cache_k.shape[2]
        past = (cache_k[l], cache_v[l], cache_logf[l])
        y_s, ks, vs, fs, cs, ffs = _trunk_layer(
            y_s, mod_s, wp, alpha, past, state_conv[l], state_ffn_conv[l],
            nb=bs, tt=ts, tq=min(past_len, 2048))

        for lst, val in zip(outs, (kp, vp, fp, cp, ffp, ks, vs, fs, cs, ffs)):
            lst.append(val)
    stacked = [jnp.stack(lst) for lst in outs]
    return (y_p, y_s, *stacked)
```

```python
import functools

import jax
import jax.numpy as jnp
from jax import lax
from jax.experimental import pallas as pl
from jax.experimental.pallas import tpu as pltpu

F32 = jnp.float32
BF16 = jnp.bfloat16

LN_EPS = 1e-5
N_MOD = 6
HEAD_DIM = 64
LANES = 128
SUBLANES = 8
HEADS_PER_BLOCK = LANES // HEAD_DIM
ATTN_HEAD_GROUP = 8
CONV_HALO = 32
FFN_HALO = 8
VMEM_LIMIT = 56 * 1024 * 1024
NEG_INIT = -1e30
LOG2E = 1.4426950408889634
N_SPLIT = 3
ONES_ROWS = 16
MIX_GROUP_ROWS = 256


def _sigmoid(x):
    return 0.5 * (jnp.tanh(0.5 * x) + 1.0)


def _log_sigmoid(x):
    return jnp.minimum(x, 0.0) - jnp.log1p(jnp.exp(-jnp.abs(x)))


def _layer_norm(x, g, b):
    mu = jnp.mean(x, axis=-1, keepdims=True)
    xc = x - mu
    var = jnp.mean(xc * xc, axis=-1, keepdims=True)
    return xc * lax.rsqrt(var + LN_EPS) * g + b


def _const_spec(shape):
    zeros = (0,) * len(shape)
    return pl.BlockSpec(shape, lambda *_: zeros)


def _params(*semantics):
    return pltpu.CompilerParams(dimension_semantics=semantics, vmem_limit_bytes=VMEM_LIMIT)


def _ada_kernel(c_ref, w_ref, b_ref, o_ref):
    o_ref[...] = jnp.dot(c_ref[...], w_ref[...], preferred_element_type=F32) + b_ref[...]


def _ada(c_all, w_ada, b_ada):
    rows, d = c_all.shape
    n = w_ada.shape[1]
    return pl.pallas_call(
        _ada_kernel,
        grid=(n // d,),
        in_specs=[pl.BlockSpec((rows, d), lambda j: (0, 0)),
                  pl.BlockSpec((d, d), lambda j: (0, j)),
                  pl.BlockSpec((1, d), lambda j: (0, j))],
        out_specs=pl.BlockSpec((rows, d), lambda j: (0, j)),
        out_shape=jax.ShapeDtypeStruct((rows, n), F32),
        compiler_params=_params("arbitrary"),
        name="ada",
    )(c_all, w_ada, b_ada.reshape(1, n))


def _inproj_kernel(x_ref, sc_ref, sh_ref, wqkv_ref, wf_ref, wglu_ref, wgate_ref, bf_ref,
                   q_ref, k32_ref, v32_ref, kb_ref, vb_ref, logf_ref, glu_ref, gate_ref,
                   *, transposed):
    nb, tt, d = x_ref.shape
    aw = kb_ref.shape[-1]
    cw = glu_ref.shape[-1]
    rows = nb * tt
    u = x_ref[...] * (1.0 + sc_ref[...]) + sh_ref[...]
    ub = u.reshape(rows, d).astype(BF16)

    def proj(w_ref, lo, hi):
        return jnp.dot(ub, w_ref[:, lo:hi], preferred_element_type=F32)

    q = proj(wqkv_ref, 0, aw) * (HEAD_DIM ** -0.5 * LOG2E)
    k = proj(wqkv_ref, aw, 2 * aw)
    kb_ref[...] = k.reshape(nb, tt, aw).astype(BF16)
    v = proj(wqkv_ref, 2 * aw, 3 * aw)
    logf = _log_sigmoid(jnp.dot(ub, wf_ref[...], preferred_element_type=F32) + bf_ref[...])
    if transposed:
        vt = v.T
        q_ref[0] = q.T.astype(BF16)
        k32_ref[0] = k.T
        v32_ref[0] = vt
        vb_ref[0] = vt.astype(BF16)
        logf_ref[0] = logf.T[:logf_ref.shape[1]]
    else:
        q_ref[...] = q.reshape(nb, tt, aw).astype(BF16)
        k32_ref[...] = k.reshape(nb, tt, aw)
        v32_ref[...] = v.reshape(nb, tt, aw)
        vb_ref[...] = v.reshape(nb, tt, aw).astype(BF16)
        nh = logf_ref.shape[-1]
        logf_ref[...] = logf[:, :nh].reshape(nb, tt, nh)

    glu = proj(wglu_ref, 0, cw) * _sigmoid(proj(wglu_ref, cw, 2 * cw))
    glu_ref[...] = glu.reshape(nb, tt, cw)

    gate = _sigmoid(jnp.dot(ub, wgate_ref[...], preferred_element_type=F32))
    gate_ref[...] = gate.reshape(nb, tt, gate.shape[-1]).astype(BF16)


def _inproj(x, sc, sh, wp, nb, tt, transposed):
    b, t, d = x.shape
    aw = wp["w_qkv"].shape[1] // 3
    cw = wp["w_glu"].shape[1] // 2
    gw = wp["w_gate"].shape[1]
    nh = wp["n_heads"]
    assert not transposed or nb == 1
    row = lambda width: pl.BlockSpec((nb, tt, width), lambda i, j: (i, j, 0))
    mod = pl.BlockSpec((nb, 1, d), lambda i, j: (i, 0, 0))
    sds = lambda width, dt: jax.ShapeDtypeStruct((b, t, width), dt)
    if transposed:
        col = lambda width: pl.BlockSpec((1, width, tt), lambda i, j: (i, 0, j))
        sds_t = lambda width, dt: jax.ShapeDtypeStruct((b, width, t), dt)
    else:
        col, sds_t = row, sds
    return pl.pallas_call(
        functools.partial(_inproj_kernel, transposed=transposed),
        grid=(b // nb, t // tt),
        in_specs=[row(d), mod, mod,
                  _const_spec(wp["w_qkv"].shape), _const_spec(wp["w_f"].shape),
                  _const_spec(wp["w_glu"].shape), _const_spec(wp["w_gate"].shape),
                  _const_spec(wp["b_f"].shape)],
        out_specs=[col(aw), col(aw), col(aw), row(aw), col(aw), col(nh), row(cw), row(gw)],
        out_shape=[sds_t(aw, BF16), sds_t(aw, F32), sds_t(aw, F32), sds(aw, BF16), sds_t(aw, BF16),
                   sds_t(nh, F32), sds(cw, F32), sds(gw, BF16)],
        compiler_params=_params("parallel", "parallel"),
        name="inproj",
    )(x, sc, sh, wp["w_qkv"], wp["w_f"], wp["w_glu"], wp["w_gate"], wp["b_f"])


def _split3(x):
    hi = x.astype(BF16)
    r1 = x - hi.astype(F32)
    mid = r1.astype(BF16)
    lo = (r1 - mid.astype(F32)).astype(BF16)
    return hi, mid, lo


def _cumsum_kernel(x_ref, o_ref, piece_ref):
    gb, r, _ = x_ref.shape
    n = gb * r
    li = lax.broadcasted_iota(jnp.int32, (LANES, LANES), 0)
    lj = lax.broadcasted_iota(jnp.int32, (LANES, LANES), 1)
    upper = jnp.where(li <= lj, 1.0, 0.0).astype(BF16)
    ri = lax.broadcasted_iota(jnp.int32, (n, n), 0)
    rj = lax.broadcasted_iota(jnp.int32, (n, n), 1)
    earlier = jnp.where((rj < ri) & (rj // r == ri // r), 1.0, 0.0).astype(BF16)
    x = x_ref[...].reshape(n, LANES)
    within = sum(jnp.dot(p, upper, preferred_element_type=F32) for p in _split3(x))
    total = jnp.broadcast_to(within[:, LANES - 1:LANES], (n, LANES))
    carry = sum(jnp.dot(earlier, p, preferred_element_type=F32) for p in _split3(total))
    f = within + carry
    o_ref[...] = f.reshape(gb, r, LANES)
    for p, piece in enumerate(_split3(f * (-LOG2E))):
        piece_ref[p] = piece.reshape(gb, r, LANES)


def _cumsum_time(logf_bht):
    b, h, l = logf_bht.shape
    tile = 2 * SUBLANES * LANES
    lp = -(-l // tile) * tile
    x = jnp.pad(logf_bht, ((0, 0), (0, 0), (0, lp - l))) if lp != l else logf_bht
    g, r = b * h, lp // LANES
    gb = SUBLANES
    f, pieces = pl.pallas_call(
        _cumsum_kernel,
        grid=(g // gb,),
        in_specs=[pl.BlockSpec((gb, r, LANES), lambda i: (i, 0, 0))],
        out_specs=[pl.BlockSpec((gb, r, LANES), lambda i: (i, 0, 0)),
                   pl.BlockSpec((N_SPLIT, gb, r, LANES), lambda i: (0, i, 0, 0))],
        out_shape=[jax.ShapeDtypeStruct((g, r, LANES), F32),
                   jax.ShapeDtypeStruct((N_SPLIT, g, r, LANES), BF16)],
        compiler_params=_params("parallel"),
        name="cumsum",
    )(x.reshape(g, r, LANES))
    return f.reshape(b, h, lp)[:, :, :l], pieces.reshape(N_SPLIT, b, h, lp)[:, :, :, :l]


def _scores(qa, k):
    return lax.dot_general(qa, k, (((1,), (1,)), ((), ())), preferred_element_type=F32)


def _attn_prompt_kernel(qt_ref, k_ref, kf_ref, vt_ref, o_ref, s_ref, acc_ref, *, tq, tk):
    group = pl.program_id(1)
    i = pl.program_id(2)
    n_heads = qt_ref.shape[1] // HEAD_DIM
    heads = range(n_heads)
    zeros = jnp.zeros((HEAD_DIM, tq), BF16)
    row = lax.broadcasted_iota(jnp.int32, (LANES, tq), 0)
    q_aug = []
    for a in heads:
        first = N_SPLIT * (n_heads * group + a)
        pick = jnp.where((row >= first) & (row < first + N_SPLIT), 1.0, 0.0).astype(BF16)
        pair = a // HEADS_PER_BLOCK
        parts = [qt_ref[0, h * HEAD_DIM:(h + 1) * HEAD_DIM, :] if h == a else zeros
                 for h in range(pair * HEADS_PER_BLOCK, (pair + 1) * HEADS_PER_BLOCK)]
        q_aug.append(jnp.concatenate(parts + [pick], axis=0))

    n_diag = tq // tk

    def produce(slot, blk, diag):
        start = pl.multiple_of(blk * tk, tk)
        kf = kf_ref[0, pl.ds(start, tk), :]
        k_aug = [jnp.concatenate([k_ref[0, pl.ds(start, tk), p * LANES:(p + 1) * LANES], kf], axis=1)
                 for p in range(n_heads // HEADS_PER_BLOCK)]
        hidden = 0 if diag is None else diag * tk
        maxima = []
        for a in heads:
            st = jnp.dot(k_aug[a // HEADS_PER_BLOCK], q_aug[a][:, hidden:],
                         preferred_element_type=F32)
            if diag is not None:
                ki = lax.broadcasted_iota(jnp.int32, st.shape, 0)
                qi = lax.broadcasted_iota(jnp.int32, st.shape, 1)
                st = jnp.where(ki <= qi, st, -jnp.inf)
                if hidden:
                    st = jnp.concatenate([jnp.full((tk, hidden), -jnp.inf, F32), st], axis=1)
            s_ref[slot, a] = st
            maxima.append(jnp.max(st.reshape(tk // SUBLANES, SUBLANES, tq), axis=0))
        return maxima

    ones = jnp.ones((ONES_ROWS, tk), BF16)

    def consume(slot, blk, m_row, m_prev, l):
        start = pl.multiple_of(blk * tk, tk)
        new_l = []
        for a in heads:
            alpha = jnp.exp2(m_prev[a] - m_row[a])
            pt = jnp.exp2(s_ref[slot, a] - m_row[a])
            vt = vt_ref[0, a * HEAD_DIM:(a + 1) * HEAD_DIM, pl.ds(start, tk)]
            pv = jnp.dot(jnp.concatenate([vt, ones], axis=0), pt.astype(BF16),
                         preferred_element_type=F32)
            acc_ref[a] = alpha * acc_ref[a] + pv[:HEAD_DIM]
            new_l.append(alpha * l[a] + pv[HEAD_DIM:HEAD_DIM + 1])
        return new_l

    def step(state, cons, prod):
        m8, m_prev, l = state
        m_row = [jnp.max(m8[a], axis=0, keepdims=True) for a in heads]
        l = consume(*cons, m_row, m_prev, l)
        if prod is not None:
            m8 = [jnp.maximum(m8[a], x) for a, x in zip(heads, produce(*prod))]
        return m8, m_row, l

    for a in heads:
        acc_ref[a] = jnp.zeros((HEAD_DIM, tq), F32)
    first_diag = n_diag * i
    m8 = produce(0, first_diag, 0)
    state = (m8, [jnp.full((1, tq), NEG_INIT, F32) for _ in heads],
             [jnp.zeros((1, tq), F32) for _ in heads])
    for dg in range(1, n_diag):
        state = step(state, ((dg - 1) % 2, first_diag + dg - 1), (dg % 2, first_diag + dg, dg))
    pending = (n_diag - 1) % 2
    assert pending == 1 and n_diag == 2

    def pair(t, carry):
        blk = 2 * t
        prev_blk = jnp.where(t == 0, first_diag + n_diag - 1, blk - 1)
        state = step(tuple(list(x) for x in carry), (1, prev_blk), (0, blk, None))
        state = step(state, (0, blk), (1, blk + 1, None))
        return tuple(tuple(x) for x in state)

    n_pairs = i * n_diag // 2
    state = lax.fori_loop(0, n_pairs // 2, lambda u, c: pair(2 * u + 1, pair(2 * u, c)),
                          tuple(tuple(x) for x in state))
    state = lax.cond(n_pairs % 2 == 1, lambda c: pair(n_pairs - 1, c), lambda c: c, state)
    last_blk = jnp.where(i == 0, first_diag + n_diag - 1, first_diag - 1)
    _, _, l = step(tuple(list(x) for x in state), (1, last_blk), None)
    out_t = jnp.concatenate([acc_ref[a] / l[a] for a in heads], axis=0)
    o_ref[0] = out_t.T.astype(o_ref.dtype)


def _attn_prompt(qt, kb, kf, vt, tq):
    b, aw, t = qt.shape
    tk = tq // 2
    gw = ATTN_HEAD_GROUP * HEAD_DIM
    return pl.pallas_call(
        functools.partial(_attn_prompt_kernel, tq=tq, tk=tk),
        scratch_shapes=[pltpu.VMEM((2, ATTN_HEAD_GROUP, tk, tq), F32),
                        pltpu.VMEM((ATTN_HEAD_GROUP, HEAD_DIM, tq), F32)],
        grid=(b, aw // gw, t // tq),
        in_specs=[pl.BlockSpec((1, gw, tq), lambda bi, g, i: (bi, g, i)),
                  pl.BlockSpec((1, t, gw), lambda bi, g, i: (bi, 0, g)),
                  pl.BlockSpec((1, t, LANES), lambda bi, g, i: (bi, 0, 0)),
                  pl.BlockSpec((1, gw, t), lambda bi, g, i: (bi, g, 0))],
        out_specs=pl.BlockSpec((1, tq, gw), lambda bi, g, i: (bi, i, g)),
        out_shape=jax.ShapeDtypeStruct((b, t, aw), BF16),
        compiler_params=_params("parallel", "parallel", "parallel"),
        name="attn_prompt",
    )(qt, kb, kf, vt)


def _attn_cached_kernel(q_ref, ckt_ref, cvt_ref, kn_ref, vn_ref, fp_ref, fn_ref, o_ref,
                        m_ref, l_ref, acc_ref):
    j = pl.program_id(1)
    n_chunks = pl.num_programs(1) - 1
    n_heads = ckt_ref.shape[1]

    @pl.when(j == 0)
    def _():
        m_ref[...] = jnp.full(m_ref.shape, NEG_INIT, F32)
        l_ref[...] = jnp.zeros(l_ref.shape, F32)
        acc_ref[...] = jnp.zeros(acc_ref.shape, F32)

    def update(h, s, pv_of):
        m_old = m_ref[h]
        m_new = jnp.maximum(m_old, jnp.max(s, axis=-1, keepdims=True))
        alpha = jnp.exp2(m_old - m_new)
        p = jnp.exp2(s - m_new)
        l_ref[h] = alpha * l_ref[h] + jnp.sum(p, axis=-1, keepdims=True)
        acc_ref[h] = alpha * acc_ref[h] + pv_of(p.astype(BF16))
        m_ref[h] = m_new

    def q_head(h):
        return q_ref[0, :, h * HEAD_DIM:(h + 1) * HEAD_DIM]

    @pl.when(j < n_chunks)
    def _():
        for h in range(n_heads):
            s = jnp.dot(q_head(h), ckt_ref[0, h].astype(BF16), preferred_element_type=F32)
            s = s - LOG2E * fp_ref[0, h:h + 1, :]
            update(h, s, lambda p, h=h: _scores(p, cvt_ref[0, h].astype(BF16)))

    @pl.when(j == n_chunks)
    def _():
        outs = []
        for h in range(n_heads):
            lanes = slice(h * HEAD_DIM, (h + 1) * HEAD_DIM)
            s = _scores(q_head(h), kn_ref[0, :, lanes]) - LOG2E * fn_ref[0, h:h + 1, :]
            qi = lax.broadcasted_iota(jnp.int32, s.shape, 0)
            ki = lax.broadcasted_iota(jnp.int32, s.shape, 1)
            s = jnp.where(ki <= qi, s, -jnp.inf)
            update(h, s, lambda p, lanes=lanes: jnp.dot(p, vn_ref[0, :, lanes],
                                                        preferred_element_type=F32))
            outs.append(acc_ref[h] / l_ref[h])
        o_ref[0] = jnp.concatenate(outs, axis=1).astype(o_ref.dtype)


def _attn_cached(q, cache_kt, cache_vt, kb, vb, f_past, f_new, chunk):
    b, t, aw = q.shape
    _, nh, dh, p = cache_kt.shape
    n_chunks = p // chunk
    last = n_chunks - 1
    new = pl.BlockSpec((1, t, aw), lambda bi, j: (bi, 0, 0))
    past = pl.BlockSpec((1, nh, dh, chunk), lambda bi, j: (bi, 0, 0, jnp.minimum(j, last)))
    return pl.pallas_call(
        _attn_cached_kernel,
        grid=(b, n_chunks + 1),
        in_specs=[new, past, past, new, new,
                  pl.BlockSpec((1, nh, chunk), lambda bi, j: (bi, 0, jnp.minimum(j, last))),
                  pl.BlockSpec((1, nh, t), lambda bi, j: (bi, 0, 0))],
        out_specs=new,
        out_shape=jax.ShapeDtypeStruct((b, t, aw), BF16),
        scratch_shapes=[pltpu.VMEM((nh, t, 1), F32), pltpu.VMEM((nh, t, 1), F32),
                        pltpu.VMEM((nh, t, dh), F32)],
        compiler_params=_params("parallel", "arbitrary"),
        name="attn_cached",
    )(q, cache_kt, cache_vt, kb, vb, f_past, f_new)


def _mix_kernel(glu_ref, hist_ref, attn_ref, gate_ref, x_ref, g1_ref,
                cw_ref, cb_ref, clg_ref, clb_ref, wap_ref, wcp_ref, wo_ref, l1g_ref, l1b_ref,
                x1_ref, sh_ref, *, alpha, row_chunk, group_rows):
    nb, tt, cw = glu_ref.shape
    d = x_ref.shape[-1]
    taps = cw_ref.shape[0]
    n_hist = hist_ref.shape[1]

    ext_ref = sh_ref.at[0]
    length = CONV_HALO + tt

    @pl.when(pl.program_id(1) == 0)
    def _():
        ext_ref[:, 0:SUBLANES, :] = jnp.zeros((nb, SUBLANES, cw), F32)
        ext_ref[:, CONV_HALO - n_hist:CONV_HALO, :] = hist_ref[...]

    ext_ref[:, CONV_HALO:length, :] = glu_ref[...]
    for r in range(1, SUBLANES):
        sh_ref[r, :, 0:length - SUBLANES, :] = ext_ref[:, r:r + length - SUBLANES, :]

    first = CONV_HALO - (taps - 1)
    def conv_rows(b, r0):
        acc = jnp.zeros((row_chunk // SUBLANES, SUBLANES, cw), F32)
        for k in range(taps):
            shift = (first + k) % SUBLANES
            base = r0 + first + k - shift
            rows_k = sh_ref[shift, b, base:base + row_chunk, :]
            acc = acc + cw_ref[k] * rows_k.reshape(row_chunk // SUBLANES, SUBLANES, cw)
        h = _layer_norm(acc.reshape(row_chunk, cw) + cb_ref[...], clg_ref[...], clb_ref[...])
        return (h * _sigmoid(h)).astype(BF16)

    for g in range(tt // group_rows):
        rs = slice(g * group_rows, (g + 1) * group_rows)
        n = nb * group_rows
        hc = jnp.concatenate([conv_rows(b, rs.start + c * row_chunk)
                              for b in range(nb) for c in range(group_rows // row_chunk)], axis=0)
        ya = jnp.dot(attn_ref[:, rs, :].reshape(n, attn_ref.shape[-1]), wap_ref[...],
                     preferred_element_type=F32)
        yb = jnp.dot(hc, wcp_ref[...], preferred_element_type=F32)
        gate = gate_ref[:, rs, :].reshape(n, 2 * d)
        merged = gate[:, :d] * ya.astype(BF16) + gate[:, d:] * yb.astype(BF16)
        z = jnp.dot(merged, wo_ref[...], preferred_element_type=F32)
        r = alpha * x_ref[:, rs, :] + g1_ref[...] * z.reshape(nb, group_rows, d)
        x1_ref[:, rs, :] = _layer_norm(r, l1g_ref[...], l1b_ref[...])

    ext_ref[:, 0:CONV_HALO, :] = ext_ref[:, tt:tt + CONV_HALO, :]


def _mix(glu, hist, attn, gate, x, g1, wp, alpha, nb, tt):
    b, t, d = x.shape
    cw = glu.shape[-1]
    row_chunk = min(tt, 64)
    group_rows = min(tt, MIX_GROUP_ROWS) if nb == 1 else tt
    row = lambda width: pl.BlockSpec((nb, tt, width), lambda i, j: (i, j, 0))
    per_seq = lambda arr: pl.BlockSpec((nb,) + arr.shape[1:], lambda i, j: (i, 0, 0))
    consts = [wp["conv_w"], wp["conv_b"], wp["conv_ln_g"], wp["conv_ln_b"], wp["w_attn_proj"],
              wp["w_conv_proj"], wp["w_out"], wp["ln1_g"], wp["ln1_b"]]
    return pl.pallas_call(
        functools.partial(_mix_kernel, alpha=alpha, row_chunk=row_chunk, group_rows=group_rows),
        grid=(b // nb, t // tt),
        in_specs=[row(cw), per_seq(hist), row(attn.shape[-1]), row(gate.shape[-1]), row(d),
                  per_seq(g1)] + [_const_spec(c.shape) for c in consts],
        out_specs=row(d),
        out_shape=jax.ShapeDtypeStruct((b, t, d), F32),
        scratch_shapes=[pltpu.VMEM((SUBLANES, nb, CONV_HALO + tt, cw), F32)],
        compiler_params=_params("parallel", "arbitrary"),
        name="mix",
    )(glu, hist, attn, gate, x, g1, *consts)


def _ffn_kernel(x1_ref, sc_ref, sh_ref, g2_ref, hist_ref, wua_ref, wuv_ref, fw_ref, fb_ref,
                wd_ref, l2g_ref, l2b_ref, y_ref, state_ref, ext_ref, *, alpha):
    nb, tt, d = x1_ref.shape
    dff = wua_ref.shape[1]
    taps = fw_ref.shape[0]
    n_hist = hist_ref.shape[1]
    rows = nb * tt

    @pl.when(pl.program_id(1) == 0)
    def _():
        ext_ref[:, FFN_HALO - n_hist:FFN_HALO, :] = hist_ref[...]

    x1 = x1_ref[...]
    u = x1 * (1.0 + sc_ref[...]) + sh_ref[...]
    ub = u.reshape(rows, d).astype(BF16)
    a2 = jnp.dot(ub, wua_ref[...], preferred_element_type=F32)
    ext_ref[:, FFN_HALO:FFN_HALO + tt, :] = a2.reshape(nb, tt, dff)
    state_ref[...] = ext_ref[:, FFN_HALO + tt - n_hist:FFN_HALO + tt, :]

    first = FFN_HALO - (taps - 1)
    conv = fb_ref[...].reshape(1, 1, dff)
    for k in range(taps):
        conv = conv + fw_ref[k:k + 1, :].reshape(1, 1, dff) * ext_ref[:, first + k:first + k + tt, :]
    v2 = jnp.dot(ub, wuv_ref[...], preferred_element_type=F32)
    conv = conv.reshape(rows, dff)
    h = conv * _sigmoid(conv) * v2

    ext_ref[:, 0:FFN_HALO, :] = ext_ref[:, tt:tt + FFN_HALO, :]

    z = jnp.dot(h.astype(BF16), wd_ref[...], preferred_element_type=F32).reshape(nb, tt, d)
    r = alpha * x1 + g2_ref[...] * z
    y_ref[...] = _layer_norm(r, l2g_ref[...], l2b_ref[...])


def _ffn(x1, sc, sh, g2, hist, wp, alpha, nb, tt):
    b, t, d = x1.shape
    dff = wp["w_up_a"].shape[1]
    n_hist = hist.shape[1]
    row = pl.BlockSpec((nb, tt, d), lambda i, j: (i, j, 0))
    per_seq = lambda arr: pl.BlockSpec((nb,) + arr.shape[1:], lambda i, j: (i, 0, 0))
    consts = [wp["w_up_a"], wp["w_up_v"], wp["ffn_conv_w"], wp["ffn_conv_b"], wp["w_down"],
              wp["ln2_g"], wp["ln2_b"]]
    return pl.pallas_call(
        functools.partial(_ffn_kernel, alpha=alpha),
        grid=(b // nb, t // tt),
        in_specs=[row, per_seq(sc), per_seq(sh), per_seq(g2), per_seq(hist)]
                 + [_const_spec(c.shape) for c in consts],
        out_specs=[row, pl.BlockSpec((nb, n_hist, dff), lambda i, j: (i, 0, 0))],
        out_shape=[jax.ShapeDtypeStruct((b, t, d), F32),
                   jax.ShapeDtypeStruct((b, n_hist, dff), F32)],
        scratch_shapes=[pltpu.VMEM((nb, FFN_HALO + tt, dff), F32)],
        compiler_params=_params("parallel", "arbitrary"),
        name="ffn",
    )(x1, sc, sh, g2, hist, *consts)


def _prepare_weights(w_in, b_f, conv_w, conv_b, conv_ln_g, conv_ln_b, w_attn_proj, w_conv_proj,
                     w_out, ln1_g, ln1_b, w_up, ffn_conv_w, ffn_conv_b, w_down, ln2_g, ln2_b):
    d = w_in.shape[0]
    aw = w_attn_proj.shape[0]
    cw = w_conv_proj.shape[0]
    nh = b_f.shape[0]
    dff = w_down.shape[0]
    s_v, s_f, s_glu = 3 * aw, 3 * aw + nh, 3 * aw + nh + 2 * cw
    row = lambda v: v.reshape(1, -1)
    return dict(
        w_qkv=w_in[:, :s_v].astype(BF16),
        w_f=jnp.pad(w_in[:, s_v:s_f], ((0, 0), (0, LANES - nh))).astype(BF16),
        w_glu=w_in[:, s_f:s_glu].astype(BF16),
        w_gate=w_in[:, s_glu:].astype(BF16),
        b_f=jnp.pad(row(b_f), ((0, 0), (0, LANES - nh))), n_heads=nh,
        conv_w=jnp.broadcast_to(conv_w[:, None, :], (conv_w.shape[0], SUBLANES, cw)),
        conv_b=row(conv_b), conv_ln_g=row(conv_ln_g), conv_ln_b=row(conv_ln_b),
        w_attn_proj=w_attn_proj.astype(BF16), w_conv_proj=w_conv_proj.astype(BF16),
        w_out=w_out.astype(BF16), ln1_g=row(ln1_g), ln1_b=row(ln1_b),
        w_up_a=w_up[:, :dff].astype(BF16), w_up_v=w_up[:, dff:].astype(BF16),
        ffn_conv_w=ffn_conv_w, ffn_conv_b=row(ffn_conv_b),
        w_down=w_down.astype(BF16), ln2_g=row(ln2_g), ln2_b=row(ln2_b),
    )


def _trunk_layer(x, mod, wp, alpha, past, conv_hist, ffn_hist, nb, tt, tq):
    b, t, d = x.shape
    sh1, sc1, g1, sh2, sc2, g2 = mod
    nh = wp["n_heads"]
    q, k32, v32, kb, vb, logf, glu, gate = _inproj(x, sc1, sh1, wp, nb, tt, transposed=past is None)

    if past is None:
        _, pieces = _cumsum_time(logf)
        kf = jnp.transpose(pieces, (1, 3, 2, 0)).reshape(b, t, -1)
        kf = jnp.pad(kf, ((0, 0), (0, 0), (0, LANES - kf.shape[-1])))
        attn = _attn_prompt(q, kb, kf, vb, tq)
        by_row = lambda a: jnp.transpose(a.reshape(b, nh, -1, t), (0, 3, 1, 2))
        k_out, v_out, logf_out = by_row(k32), by_row(v32), jnp.transpose(logf, (0, 2, 1))
    else:
        cache_k, cache_v, cache_logf = past
        p = cache_k.shape[1]
        f_all, _ = _cumsum_time(jnp.concatenate(
            [jnp.transpose(cache_logf, (0, 2, 1)), jnp.transpose(logf, (0, 2, 1))], axis=2))
        by_feature = lambda a: jnp.transpose(a, (0, 2, 3, 1))
        attn = _attn_cached(q, by_feature(cache_k), by_feature(cache_v), kb, vb,
                            f_all[:, :, :p], f_all[:, :, p:], tq)
        heads = lambda a: a.reshape(b, t, nh, -1)
        k_out, v_out, logf_out = heads(k32), heads(v32), logf

    x1 = _mix(glu, conv_hist, attn, gate, x, g1, wp, alpha, nb, tt)
    y, ffn_state = _ffn(x1, sc2, sh2, g2, ffn_hist, wp, alpha, nb, tt)
    conv_state = glu[:, t - conv_hist.shape[1]:, :]
    return y, k_out, v_out, logf_out, conv_state, ffn_state


def kernel(x_prompt, x_sample, c_prompt, c_sample, cache_k, cache_v, cache_logf, state_conv, state_ffn_conv, w_ada, b_ada, w_in, b_f, conv_w, conv_b, conv_ln_g, conv_ln_b, w_attn_proj, w_conv_proj, w_out, ln1_g, ln1_b, w_up, ffn_conv_w, ffn_conv_b, w_down, ln2_g, ln2_b):
    depth = w_ada.shape[0]
    bp, tp, d = x_prompt.shape
    bs, ts, _ = x_sample.shape
    nh = b_f.shape[1]
    alpha = float((2 * depth) ** 0.25)
    prompt_tile = min(tp, 512)
    prompt_tq = min(tp, 512)

    y_p, y_s = x_prompt, x_sample
    c_all = jnp.concatenate([c_prompt, c_sample], axis=0)
    outs = [[] for _ in range(10)]
    for l in range(depth):
        wp = _prepare_weights(w_in[l], b_f[l], conv_w[l], conv_b[l], conv_ln_g[l], conv_ln_b[l],
                              w_attn_proj[l], w_conv_proj[l], w_out[l], ln1_g[l], ln1_b[l],
                              w_up[l], ffn_conv_w[l], ffn_conv_b[l], w_down[l], ln2_g[l], ln2_b[l])
        mod = _ada(c_all, w_ada[l], b_ada[l])
        mod_p = [mod[:bp, i * d:(i + 1) * d].reshape(bp, 1, d) for i in range(N_MOD)]
        mod_s = [mod[bp:, i * d:(i + 1) * d].reshape(bs, 1, d) for i in range(N_MOD)]

        zc = jnp.zeros((bp,) + state_conv.shape[2:], F32)
        zf = jnp.zeros((bp,) + state_ffn_conv.shape[2:], F32)
        y_p, kp, vp, fp, cp, ffp = _trunk_layer(
            y_p, mod_p, wp, alpha, None, zc, zf, nb=1, tt=prompt_tile, tq=prompt_tq)

        past_len = cache_k.shape[2]
        past = (cache_k[l], cache_v[l], cache_logf[l])
        y_s, ks, vs, fs, cs, ffs = _trunk_layer(
            y_s, mod_s, wp, alpha, past, state_conv[l], state_ffn_conv[l],
            nb=bs, tt=ts, tq=min(past_len, 2048))

        for lst, val in zip(outs, (kp, vp, fp, cp, ffp, ks, vs, fs, cs, ffs)):
            lst.append(val)
    stacked = [jnp.stack(lst) for lst in outs]
    return (y_p, y_s, *stacked)
```

```python
import functools

import jax
import jax.numpy as jnp
from jax import lax
from jax.experimental import pallas as pl
from jax.experimental.pallas import tpu as pltpu

F32 = jnp.float32
BF16 = jnp.bfloat16

LN_EPS = 1e-5
N_MOD = 6
HEAD_DIM = 64
LANES = 128
SUBLANES = 8
HEADS_PER_BLOCK = LANES // HEAD_DIM
ATTN_HEAD_GROUP = 8
CONV_HALO = 32
FFN_HALO = 8
VMEM_LIMIT = 56 * 1024 * 1024
NEG_INIT = -1e30
LOG2E = 1.4426950408889634
N_SPLIT = 3
ONES_ROWS = 16
MIX_GROUP_ROWS = 256


def _sigmoid(x):
    return 0.5 * (jnp.tanh(0.5 * x) + 1.0)


def _log_sigmoid(x):
    return jnp.minimum(x, 0.0) - jnp.log1p(jnp.exp(-jnp.abs(x)))


def _layer_norm(x, g, b):
    mu = jnp.mean(x, axis=-1, keepdims=True)
    xc = x - mu
    var = jnp.mean(xc * xc, axis=-1, keepdims=True)
    return xc * lax.rsqrt(var + LN_EPS) * g + b


def _const_spec(shape):
    zeros = (0,) * len(shape)
    return pl.BlockSpec(shape, lambda *_: zeros)


def _params(*semantics):
    return pltpu.CompilerParams(dimension_semantics=semantics, vmem_limit_bytes=VMEM_LIMIT)


def _ada_kernel(c_ref, w_ref, b_ref, o_ref):
    o_ref[...] = jnp.dot(c_ref[...], w_ref[...], preferred_element_type=F32) + b_ref[...]


def _ada(c_all, w_ada, b_ada):
    rows, d = c_all.shape
    n = w_ada.shape[1]
    return pl.pallas_call(
        _ada_kernel,
        grid=(n // d,),
        in_specs=[pl.BlockSpec((rows, d), lambda j: (0, 0)),
                  pl.BlockSpec((d, d), lambda j: (0, j)),
                  pl.BlockSpec((1, d), lambda j: (0, j))],
        out_specs=pl.BlockSpec((rows, d), lambda j: (0, j)),
        out_shape=jax.ShapeDtypeStruct((rows, n), F32),
        compiler_params=_params("arbitrary"),
        name="ada",
    )(c_all, w_ada, b_ada.reshape(1, n))


def _inproj_kernel(x_ref, sc_ref, sh_ref, wqkv_ref, wf_ref, wglu_ref, wgate_ref, bf_ref,
                   q_ref, k32_ref, v32_ref, kb_ref, vb_ref, logf_ref, glu_ref, gate_ref,
                   *, transposed):
    nb, tt, d = x_ref.shape
    aw = kb_ref.shape[-1]
    cw = glu_ref.shape[-1]
    rows = nb * tt
    u = x_ref[...] * (1.0 + sc_ref[...]) + sh_ref[...]
    ub = u.reshape(rows, d).astype(BF16)

    def proj(w_ref, lo, hi):
        return jnp.dot(ub, w_ref[:, lo:hi], preferred_element_type=F32)

    q = proj(wqkv_ref, 0, aw) * (HEAD_DIM ** -0.5 * LOG2E)
    k = proj(wqkv_ref, aw, 2 * aw)
    kb_ref[...] = k.reshape(nb, tt, aw).astype(BF16)
    v = proj(wqkv_ref, 2 * aw, 3 * aw)
    logf = _log_sigmoid(jnp.dot(ub, wf_ref[...], preferred_element_type=F32) + bf_ref[...])
    if transposed:
        vt = v.T
        q_ref[0] = q.T.astype(BF16)
        k32_ref[0] = k.T
        v32_ref[0] = vt
        vb_ref[0] = vt.astype(BF16)
        logf_ref[0] = logf.T[:logf_ref.shape[1]]
    else:
        q_ref[...] = q.reshape(nb, tt, aw).astype(BF16)
        k32_ref[...] = k.reshape(nb, tt, aw)
        v32_ref[...] = v.reshape(nb, tt, aw)
        vb_ref[...] = v.reshape(nb, tt, aw).astype(BF16)
        nh = logf_ref.shape[-1]
        logf_ref[...] = logf[:, :nh].reshape(nb, tt, nh)

    glu = proj(wglu_ref, 0, cw) * _sigmoid(proj(wglu_ref, cw, 2 * cw))
    glu_ref[...] = glu.reshape(nb, tt, cw)

    gate = _sigmoid(jnp.dot(ub, wgate_ref[...], preferred_element_type=F32))
    gate_ref[...] = gate.reshape(nb, tt, gate.shape[-1]).astype(BF16)


def _inproj(x, sc, sh, wp, nb, tt, transposed):
    b, t, d = x.shape
    aw = wp["w_qkv"].shape[1] // 3
    cw = wp["w_glu"].shape[1] // 2
    gw = wp["w_gate"].shape[1]
    nh = wp["n_heads"]
    assert not transposed or nb == 1
    row = lambda width: pl.BlockSpec((nb, tt, width), lambda i, j: (i, j, 0))
    mod = pl.BlockSpec((nb, 1, d), lambda i, j: (i, 0, 0))
    sds = lambda width, dt: jax.ShapeDtypeStruct((b, t, width), dt)
    if transposed:
        col = lambda width: pl.BlockSpec((1, width, tt), lambda i, j: (i, 0, j))
        sds_t = lambda width, dt: jax.ShapeDtypeStruct((b, width, t), dt)
    else:
        col, sds_t = row, sds
    return pl.pallas_call(
        functools.partial(_inproj_kernel, transposed=transposed),
        grid=(b // nb, t // tt),
        in_specs=[row(d), mod, mod,
                  _const_spec(wp["w_qkv"].shape), _const_spec(wp["w_f"].shape),
                  _const_spec(wp["w_glu"].shape), _const_spec(wp["w_gate"].shape),
                  _const_spec(wp["b_f"].shape)],
        out_specs=[col(aw), col(aw), col(aw), row(aw), col(aw), col(nh), row(cw), row(gw)],
        out_shape=[sds_t(aw, BF16), sds_t(aw, F32), sds_t(aw, F32), sds(aw, BF16), sds_t(aw, BF16),
                   sds_t(nh, F32), sds(cw, F32), sds(gw, BF16)],
        compiler_params=_params("parallel", "parallel"),
        name="inproj",
    )(x, sc, sh, wp["w_qkv"], wp["w_f"], wp["w_glu"], wp["w_gate"], wp["b_f"])


def _split3(x):
    hi = x.astype(BF16)
    r1 = x - hi.astype(F32)
    mid = r1.astype(BF16)
    lo = (r1 - mid.astype(F32)).astype(BF16)
    return hi, mid, lo


def _cumsum_kernel(x_ref, o_ref, piece_ref):
    gb, r, _ = x_ref.shape
    n = gb * r
    li = lax.broadcasted_iota(jnp.int32, (LANES, LANES), 0)
    lj = lax.broadcasted_iota(jnp.int32, (LANES, LANES), 1)
    upper = jnp.where(li <= lj, 1.0, 0.0).astype(BF16)
    ri = lax.broadcasted_iota(jnp.int32, (n, n), 0)
    rj = lax.broadcasted_iota(jnp.int32, (n, n), 1)
    earlier = jnp.where((rj < ri) & (rj // r == ri // r), 1.0, 0.0).astype(BF16)
    x = x_ref[...].reshape(n, LANES)
    within = sum(jnp.dot(p, upper, preferred_element_type=F32) for p in _split3(x))
    total = jnp.broadcast_to(within[:, LANES - 1:LANES], (n, LANES))
    carry = sum(jnp.dot(earlier, p, preferred_element_type=F32) for p in _split3(total))
    f = within + carry
    o_ref[...] = f.reshape(gb, r, LANES)
    for p, piece in enumerate(_split3(f * (-LOG2E))):
        piece_ref[p] = piece.reshape(gb, r, LANES)


def _cumsum_time(logf_bht):
    b, h, l = logf_bht.shape
    tile = 2 * SUBLANES * LANES
    lp = -(-l // tile) * tile
    x = jnp.pad(logf_bht, ((0, 0), (0, 0), (0, lp - l))) if lp != l else logf_bht
    g, r = b * h, lp // LANES
    gb = SUBLANES
    f, pieces = pl.pallas_call(
        _cumsum_kernel,
        grid=(g // gb,),
        in_specs=[pl.BlockSpec((gb, r, LANES), lambda i: (i, 0, 0))],
        out_specs=[pl.BlockSpec((gb, r, LANES), lambda i: (i, 0, 0)),
                   pl.BlockSpec((N_SPLIT, gb, r, LANES), lambda i: (0, i, 0, 0))],
        out_shape=[jax.ShapeDtypeStruct((g, r, LANES), F32),
                   jax.ShapeDtypeStruct((N_SPLIT, g, r, LANES), BF16)],
        compiler_params=_params("parallel"),
        name="cumsum",
    )(x.reshape(g, r, LANES))
    return f.reshape(b, h, lp)[:, :, :l], pieces.reshape(N_SPLIT, b, h, lp)[:, :, :, :l]


def _scores(qa, k):
    return lax.dot_general(qa, k, (((1,), (1,)), ((), ())), preferred_element_type=F32)


def _attn_prompt_kernel(qt_ref, k_ref, kf_ref, vt_ref, o_ref, s_ref, acc_ref, *, tq, tk):
    group = pl.program_id(1)
    i = pl.program_id(2)
    n_heads = qt_ref.shape[1] // HEAD_DIM
    heads = range(n_heads)
    zeros = jnp.zeros((HEAD_DIM, tq), BF16)
    row = lax.broadcasted_iota(jnp.int32, (LANES, tq), 0)
    q_aug = []
    for a in heads:
        first = N_SPLIT * (n_heads * group + a)
        pick = jnp.where((row >= first) & (row < first + N_SPLIT), 1.0, 0.0).astype(BF16)
        pair = a // HEADS_PER_BLOCK
        parts = [qt_ref[0, h * HEAD_DIM:(h + 1) * HEAD_DIM, :] if h == a else zeros
                 for h in range(pair * HEADS_PER_BLOCK, (pair + 1) * HEADS_PER_BLOCK)]
        q_aug.append(jnp.concatenate(parts + [pick], axis=0))

    n_diag = tq // tk

    def produce(slot, blk, diag):
        start = pl.multiple_of(blk * tk, tk)
        kf = kf_ref[0, pl.ds(start, tk), :]
        k_aug = [jnp.concatenate([k_ref[0, pl.ds(start, tk), p * LANES:(p + 1) * LANES], kf], axis=1)
                 for p in range(n_heads // HEADS_PER_BLOCK)]
        hidden = 0 if diag is None else diag * tk
        maxima = []
        for a in heads:
            st = jnp.dot(k_aug[a // HEADS_PER_BLOCK], q_aug[a][:, hidden:],
                         preferred_element_type=F32)
            if diag is not None:
                ki = lax.broadcasted_iota(jnp.int32, st.shape, 0)
                qi = lax.broadcasted_iota(jnp.int32, st.shape, 1)
                st = jnp.where(ki <= qi, st, -jnp.inf)
                if hidden:
                    st = jnp.concatenate([jnp.full((tk, hidden), -jnp.inf, F32), st], axis=1)
            s_ref[slot, a] = st
            maxima.append(jnp.max(st.reshape(tk // SUBLANES, SUBLANES, tq), axis=0))
        return maxima

    ones = jnp.ones((ONES_ROWS, tk), BF16)

    def consume(slot, blk, m_row, m_prev, l):
        start = pl.multiple_of(blk * tk, tk)
        new_l = []
        for a in heads:
            alpha = jnp.exp2(m_prev[a] - m_row[a])
            pt = jnp.exp2(s_ref[slot, a] - m_row[a])
            vt = vt_ref[0, a * HEAD_DIM:(a + 1) * HEAD_DIM, pl.ds(start, tk)]
            pv = jnp.dot(jnp.concatenate([vt, ones], axis=0), pt.astype(BF16),
                         preferred_element_type=F32)
            acc_ref[a] = alpha * acc_ref[a] + pv[:HEAD_DIM]
            new_l.append(alpha * l[a] + pv[HEAD_DIM:HEAD_DIM + 1])
        return new_l

    def step(state, cons, prod):
        m8, m_prev, l = state
        m_row = [jnp.max(m8[a], axis=0, keepdims=True) for a in heads]
        if prod is not None:
            m8 = [jnp.maximum(m8[a], x) for a, x in zip(heads, produce(*prod))]
        l = consume(*cons, m_row, m_prev, l)
        return m8, m_row, l

    for a in heads:
        acc_ref[a] = jnp.zeros((HEAD_DIM, tq), F32)
    first_diag = n_diag * i
    m8 = produce(0, first_diag, 0)
    state = (m8, [jnp.full((1, tq), NEG_INIT, F32) for _ in heads],
             [jnp.zeros((1, tq), F32) for _ in heads])
    for dg in range(1, n_diag):
        state = step(state, ((dg - 1) % 2, first_diag + dg - 1), (dg % 2, first_diag + dg, dg))
    pending = (n_diag - 1) % 2
    assert pending == 1 and n_diag == 2

    def pair(t, carry):
        blk = 2 * t
        prev_blk = jnp.where(t == 0, first_diag + n_diag - 1, blk - 1)
        state = step(tuple(list(x) for x in carry), (1, prev_blk), (0, blk, None))
        state = step(state, (0, blk), (1, blk + 1, None))
        return tuple(tuple(x) for x in state)

    n_pairs = i * n_diag // 2
    state = lax.fori_loop(0, n_pairs // 2, lambda u, c: pair(2 * u + 1, pair(2 * u, c)),
                          tuple(tuple(x) for x in state))
    state = lax.cond(n_pairs % 2 == 1, lambda c: pair(n_pairs - 1, c), lambda c: c, state)
    last_blk = jnp.where(i == 0, first_diag + n_diag - 1, first_diag - 1)
    _, _, l = step(tuple(list(x) for x in state), (1, last_blk), None)
    out_t = jnp.concatenate([acc_ref[a] / l[a] for a in heads], axis=0)
    o_ref[0] = out_t.T.astype(o_ref.dtype)


def _attn_prompt(qt, kb, kf, vt, tq):
    b, aw, t = qt.shape
    tk = tq // 2
    gw = ATTN_HEAD_GROUP * HEAD_DIM
    return pl.pallas_call(
        functools.partial(_attn_prompt_kernel, tq=tq, tk=tk),
        scratch_shapes=[pltpu.VMEM((2, ATTN_HEAD_GROUP, tk, tq), F32),
                        pltpu.VMEM((ATTN_HEAD_GROUP, HEAD_DIM, tq), F32)],
        grid=(b, aw // gw, t // tq),
        in_specs=[pl.BlockSpec((1, gw, tq), lambda bi, g, i: (bi, g, i)),
                  pl.BlockSpec((1, t, gw), lambda bi, g, i: (bi, 0, g)),
                  pl.BlockSpec((1, t, LANES), lambda bi, g, i: (bi, 0, 0)),
                  pl.BlockSpec((1, gw, t), lambda bi, g, i: (bi, g, 0))],
        out_specs=pl.BlockSpec((1, tq, gw), lambda bi, g, i: (bi, i, g)),
        out_shape=jax.ShapeDtypeStruct((b, t, aw), BF16),
        compiler_params=_params("parallel", "parallel", "parallel"),
        name="attn_prompt",
    )(qt, kb, kf, vt)


def _attn_cached_kernel(q_ref, ckt_ref, cvt_ref, kn_ref, vn_ref, fp_ref, fn_ref, o_ref,
                        m_ref, l_ref, acc_ref):
    j = pl.program_id(1)
    n_chunks = pl.num_programs(1) - 1
    n_heads = ckt_ref.shape[1]

    @pl.when(j == 0)
    def _():
        m_ref[...] = jnp.full(m_ref.shape, NEG_INIT, F32)
        l_ref[...] = jnp.zeros(l_ref.shape, F32)
        acc_ref[...] = jnp.zeros(acc_ref.shape, F32)

    def update(h, s, pv_of):
        m_old = m_ref[h]
        m_new = jnp.maximum(m_old, jnp.max(s, axis=-1, keepdims=True))
        alpha = jnp.exp2(m_old - m_new)
        p = jnp.exp2(s - m_new)
        l_ref[h] = alpha * l_ref[h] + jnp.sum(p, axis=-1, keepdims=True)
        acc_ref[h] = alpha * acc_ref[h] + pv_of(p.astype(BF16))
        m_ref[h] = m_new

    def q_head(h):
        return q_ref[0, :, h * HEAD_DIM:(h + 1) * HEAD_DIM]

    @pl.when(j < n_chunks)
    def _():
        for h in range(n_heads):
            s = jnp.dot(q_head(h), ckt_ref[0, h].astype(BF16), preferred_element_type=F32)
            s = s - LOG2E * fp_ref[0, h:h + 1, :]
            update(h, s, lambda p, h=h: _scores(p, cvt_ref[0, h].astype(BF16)))

    @pl.when(j == n_chunks)
    def _():
        outs = []
        for h in range(n_heads):
            lanes = slice(h * HEAD_DIM, (h + 1) * HEAD_DIM)
            s = _scores(q_head(h), kn_ref[0, :, lanes]) - LOG2E * fn_ref[0, h:h + 1, :]
            qi = lax.broadcasted_iota(jnp.int32, s.shape, 0)
            ki = lax.broadcasted_iota(jnp.int32, s.shape, 1)
            s = jnp.where(ki <= qi, s, -jnp.inf)
            update(h, s, lambda p, lanes=lanes: jnp.dot(p, vn_ref[0, :, lanes],
                                                        preferred_element_type=F32))
            outs.append(acc_ref[h] / l_ref[h])
        o_ref[0] = jnp.concatenate(outs, axis=1).astype(o_ref.dtype)


def _attn_cached(q, cache_kt, cache_vt, kb, vb, f_past, f_new, chunk):
    b, t, aw = q.shape
    _, nh, dh, p = cache_kt.shape
    n_chunks = p // chunk
    last = n_chunks - 1
    new = pl.BlockSpec((1, t, aw), lambda bi, j: (bi, 0, 0))
    past = pl.BlockSpec((1, nh, dh, chunk), lambda bi, j: (bi, 0, 0, jnp.minimum(j, last)))
    return pl.pallas_call(
        _attn_cached_kernel,
        grid=(b, n_chunks + 1),
        in_specs=[new, past, past, new, new,
                  pl.BlockSpec((1, nh, chunk), lambda bi, j: (bi, 0, jnp.minimum(j, last))),
                  pl.BlockSpec((1, nh, t), lambda bi, j: (bi, 0, 0))],
        out_specs=new,
        out_shape=jax.ShapeDtypeStruct((b, t, aw), BF16),
        scratch_shapes=[pltpu.VMEM((nh, t, 1), F32), pltpu.VMEM((nh, t, 1), F32),
                        pltpu.VMEM((nh, t, dh), F32)],
        compiler_params=_params("parallel", "arbitrary"),
        name="attn_cached",
    )(q, cache_kt, cache_vt, kb, vb, f_past, f_new)


def _mix_kernel(glu_ref, hist_ref, attn_ref, gate_ref, x_ref, g1_ref,
                cw_ref, cb_ref, clg_ref, clb_ref, wap_ref, wcp_ref, wo_ref, l1g_ref, l1b_ref,
                x1_ref, sh_ref, *, alpha, row_chunk, group_rows):
    nb, tt, cw = glu_ref.shape
    d = x_ref.shape[-1]
    taps = cw_ref.shape[0]
    n_hist = hist_ref.shape[1]

    ext_ref = sh_ref.at[0]
    length = CONV_HALO + tt

    @pl.when(pl.program_id(1) == 0)
    def _():
        ext_ref[:, 0:SUBLANES, :] = jnp.zeros((nb, SUBLANES, cw), F32)
        ext_ref[:, CONV_HALO - n_hist:CONV_HALO, :] = hist_ref[...]

    ext_ref[:, CONV_HALO:length, :] = glu_ref[...]
    for r in range(1, SUBLANES):
        sh_ref[r, :, 0:length - SUBLANES, :] = ext_ref[:, r:r + length - SUBLANES, :]

    first = CONV_HALO - (taps - 1)
    def conv_rows(b, r0):
        acc = jnp.zeros((row_chunk // SUBLANES, SUBLANES, cw), F32)
        for k in range(taps):
            shift = (first + k) % SUBLANES
            base = r0 + first + k - shift
            rows_k = sh_ref[shift, b, base:base + row_chunk, :]
            acc = acc + cw_ref[k] * rows_k.reshape(row_chunk // SUBLANES, SUBLANES, cw)
        h = _layer_norm(acc.reshape(row_chunk, cw) + cb_ref[...], clg_ref[...], clb_ref[...])
        return (h * _sigmoid(h)).astype(BF16)

    for g in range(tt // group_rows):
        rs = slice(g * group_rows, (g + 1) * group_rows)
        n = nb * group_rows
        hc = jnp.concatenate([conv_rows(b, rs.start + c * row_chunk)
                              for b in range(nb) for c in range(group_rows // row_chunk)], axis=0)
        ya = jnp.dot(attn_ref[:, rs, :].reshape(n, attn_ref.shape[-1]), wap_ref[...],
                     preferred_element_type=F32)
        yb = jnp.dot(hc, wcp_ref[...], preferred_element_type=F32)
        gate = gate_ref[:, rs, :].reshape(n, 2 * d)
        merged = gate[:, :d] * ya.astype(BF16) + gate[:, d:] * yb.astype(BF16)
        z = jnp.dot(merged, wo_ref[...], preferred_element_type=F32)
        r = alpha * x_ref[:, rs, :] + g1_ref[...] * z.reshape(nb, group_rows, d)
        x1_ref[:, rs, :] = _layer_norm(r, l1g_ref[...], l1b_ref[...])

    ext_ref[:, 0:CONV_HALO, :] = ext_ref[:, tt:tt + CONV_HALO, :]


def _mix(glu, hist, attn, gate, x, g1, wp, alpha, nb, tt):
    b, t, d = x.shape
    cw = glu.shape[-1]
    row_chunk = min(tt, 64)
    group_rows = min(tt, MIX_GROUP_ROWS) if nb == 1 else tt
    row = lambda width: pl.BlockSpec((nb, tt, width), lambda i, j: (i, j, 0))
    per_seq = lambda arr: pl.BlockSpec((nb,) + arr.shape[1:], lambda i, j: (i, 0, 0))
    consts = [wp["conv_w"], wp["conv_b"], wp["conv_ln_g"], wp["conv_ln_b"], wp["w_attn_proj"],
              wp["w_conv_proj"], wp["w_out"], wp["ln1_g"], wp["ln1_b"]]
    return pl.pallas_call(
        functools.partial(_mix_kernel, alpha=alpha, row_chunk=row_chunk, group_rows=group_rows),
        grid=(b // nb, t // tt),
        in_specs=[row(cw), per_seq(hist), row(attn.shape[-1]), row(gate.shape[-1]), row(d),
                  per_seq(g1)] + [_const_spec(c.shape) for c in consts],
        out_specs=row(d),
        out_shape=jax.ShapeDtypeStruct((b, t, d), F32),
        scratch_shapes=[pltpu.VMEM((SUBLANES, nb, CONV_HALO + tt, cw), F32)],
        compiler_params=_params("parallel", "arbitrary"),
        name="mix",
    )(glu, hist, attn, gate, x, g1, *consts)


def _ffn_kernel(x1_ref, sc_ref, sh_ref, g2_ref, hist_ref, wua_ref, wuv_ref, fw_ref, fb_ref,
                wd_ref, l2g_ref, l2b_ref, y_ref, state_ref, ext_ref, *, alpha):
    nb, tt, d = x1_ref.shape
    dff = wua_ref.shape[1]
    taps = fw_ref.shape[0]
    n_hist = hist_ref.shape[1]
    rows = nb * tt

    @pl.when(pl.program_id(1) == 0)
    def _():
        ext_ref[:, FFN_HALO - n_hist:FFN_HALO, :] = hist_ref[...]

    x1 = x1_ref[...]
    u = x1 * (1.0 + sc_ref[...]) + sh_ref[...]
    ub = u.reshape(rows, d).astype(BF16)
    a2 = jnp.dot(ub, wua_ref[...], preferred_element_type=F32)
    ext_ref[:, FFN_HALO:FFN_HALO + tt, :] = a2.reshape(nb, tt, dff)
    state_ref[...] = ext_ref[:, FFN_HALO + tt - n_hist:FFN_HALO + tt, :]

    first = FFN_HALO - (taps - 1)
    conv = fb_ref[...].reshape(1, 1, dff)
    for k in range(taps):
        conv = conv + fw_ref[k:k + 1, :].reshape(1, 1, dff) * ext_ref[:, first + k:first + k + tt, :]
    v2 = jnp.dot(ub, wuv_ref[...], preferred_element_type=F32)
    conv = conv.reshape(rows, dff)
    h = conv * _sigmoid(conv) * v2

    ext_ref[:, 0:FFN_HALO, :] = ext_ref[:, tt:tt + FFN_HALO, :]

    z = jnp.dot(h.astype(BF16), wd_ref[...], preferred_element_type=F32).reshape(nb, tt, d)
    r = alpha * x1 + g2_ref[...] * z
    y_ref[...] = _layer_norm(r, l2g_ref[...], l2b_ref[...])


def _ffn(x1, sc, sh, g2, hist, wp, alpha, nb, tt):
    b, t, d = x1.shape
    dff = wp["w_up_a"].shape[1]
    n_hist = hist.shape[1]
    row = pl.BlockSpec((nb, tt, d), lambda i, j: (i, j, 0))
    per_seq = lambda arr: pl.BlockSpec((nb,) + arr.shape[1:], lambda i, j: (i, 0, 0))
    consts = [wp["w_up_a"], wp["w_up_v"], wp["ffn_conv_w"], wp["ffn_conv_b"], wp["w_down"],
              wp["ln2_g"], wp["ln2_b"]]
    return pl.pallas_call(
        functools.partial(_ffn_kernel, alpha=alpha),
        grid=(b // nb, t // tt),
        in_specs=[row, per_seq(sc), per_seq(sh), per_seq(g2), per_seq(hist)]
                 + [_const_spec(c.shape) for c in consts],
        out_specs=[row, pl.BlockSpec((nb, n_hist, dff), lambda i, j: (i, 0, 0))],
        out_shape=[jax.ShapeDtypeStruct((b, t, d), F32),
                   jax.ShapeDtypeStruct((b, n_hist, dff), F32)],
        scratch_shapes=[pltpu.VMEM((nb, FFN_HALO + tt, dff), F32)],
        compiler_params=_params("parallel", "arbitrary"),
        name="ffn",
    )(x1, sc, sh, g2, hist, *consts)


def _prepare_weights(w_in, b_f, conv_w, conv_b, conv_ln_g, conv_ln_b, w_attn_proj, w_conv_proj,
                     w_out, ln1_g, ln1_b, w_up, ffn_conv_w, ffn_conv_b, w_down, ln2_g, ln2_b):
    d = w_in.shape[0]
    aw = w_attn_proj.shape[0]
    cw = w_conv_proj.shape[0]
    nh = b_f.shape[0]
    dff = w_down.shape[0]
    s_v, s_f, s_glu = 3 * aw, 3 * aw + nh, 3 * aw + nh + 2 * cw
    row = lambda v: v.reshape(1, -1)
    return dict(
        w_qkv=w_in[:, :s_v].astype(BF16),
        w_f=jnp.pad(w_in[:, s_v:s_f], ((0, 0), (0, LANES - nh))).astype(BF16),
        w_glu=w_in[:, s_f:s_glu].astype(BF16),
        w_gate=w_in[:, s_glu:].astype(BF16),
        b_f=jnp.pad(row(b_f), ((0, 0), (0, LANES - nh))), n_heads=nh,
        conv_w=jnp.broadcast_to(conv_w[:, None, :], (conv_w.shape[0], SUBLANES, cw)),
        conv_b=row(conv_b), conv_ln_g=row(conv_ln_g), conv_ln_b=row(conv_ln_b),
        w_attn_proj=w_attn_proj.astype(BF16), w_conv_proj=w_conv_proj.astype(BF16),
        w_out=w_out.astype(BF16), ln1_g=row(ln1_g), ln1_b=row(ln1_b),
        w_up_a=w_up[:, :dff].astype(BF16), w_up_v=w_up[:, dff:].astype(BF16),
        ffn_conv_w=ffn_conv_w, ffn_conv_b=row(ffn_conv_b),
        w_down=w_down.astype(BF16), ln2_g=row(ln2_g), ln2_b=row(ln2_b),
    )


def _trunk_layer(x, mod, wp, alpha, past, conv_hist, ffn_hist, nb, tt, tq):
    b, t, d = x.shape
    sh1, sc1, g1, sh2, sc2, g2 = mod
    nh = wp["n_heads"]
    q, k32, v32, kb, vb, logf, glu, gate = _inproj(x, sc1, sh1, wp, nb, tt, transposed=past is None)

    if past is None:
        _, pieces = _cumsum_time(logf)
        kf = jnp.transpose(pieces, (1, 3, 2, 0)).reshape(b, t, -1)
        kf = jnp.pad(kf, ((0, 0), (0, 0), (0, LANES - kf.shape[-1])))
        attn = _attn_prompt(q, kb, kf, vb, tq)
        by_row = lambda a: jnp.transpose(a.reshape(b, nh, -1, t), (0, 3, 1, 2))
        k_out, v_out, logf_out = by_row(k32), by_row(v32), jnp.transpose(logf, (0, 2, 1))
    else:
        cache_k, cache_v, cache_logf = past
        p = cache_k.shape[1]
        f_all, _ = _cumsum_time(jnp.concatenate(
            [jnp.transpose(cache_logf, (0, 2, 1)), jnp.transpose(logf, (0, 2, 1))], axis=2))
        by_feature = lambda a: jnp.transpose(a, (0, 2, 3, 1))
        attn = _attn_cached(q, by_feature(cache_k), by_feature(cache_v), kb, vb,
                            f_all[:, :, :p], f_all[:, :, p:], tq)
        heads = lambda a: a.reshape(b, t, nh, -1)
        k_out, v_out, logf_out = heads(k32), heads(v32), logf

    x1 = _mix(glu, conv_hist, attn, gate, x, g1, wp, alpha, nb, tt)
    y, ffn_state = _ffn(x1, sc2, sh2, g2, ffn_hist, wp, alpha, nb, tt)
    conv_state = glu[:, t - conv_hist.shape[1]:, :]
    return y, k_out, v_out, logf_out, conv_state, ffn_state


def kernel(x_prompt, x_sample, c_prompt, c_sample, cache_k, cache_v, cache_logf, state_conv, state_ffn_conv, w_ada, b_ada, w_in, b_f, conv_w, conv_b, conv_ln_g, conv_ln_b, w_attn_proj, w_conv_proj, w_out, ln1_g, ln1_b, w_up, ffn_conv_w, ffn_conv_b, w_down, ln2_g, ln2_b):
    depth = w_ada.shape[0]
    bp, tp, d = x_prompt.shape
    bs, ts, _ = x_sample.shape
    nh = b_f.shape[1]
    alpha = float((2 * depth) ** 0.25)
    prompt_tile = min(tp, 512)
    prompt_tq = min(tp, 512)

    y_p, y_s = x_prompt, x_sample
    c_all = jnp.concatenate([c_prompt, c_sample], axis=0)
    outs = [[] for _ in range(10)]
    for l in range(depth):
        wp = _prepare_weights(w_in[l], b_f[l], conv_w[l], conv_b[l], conv_ln_g[l], conv_ln_b[l],
                              w_attn_proj[l], w_conv_proj[l], w_out[l], ln1_g[l], ln1_b[l],
                              w_up[l], ffn_conv_w[l], ffn_conv_b[l], w_down[l], ln2_g[l], ln2_b[l])
        mod = _ada(c_all, w_ada[l], b_ada[l])
        mod_p = [mod[:bp, i * d:(i + 1) * d].reshape(bp, 1, d) for i in range(N_MOD)]
        mod_s = [mod[bp:, i * d:(i + 1) * d].reshape(bs, 1, d) for i in range(N_MOD)]

        zc = jnp.zeros((bp,) + state_conv.shape[2:], F32)
        zf = jnp.zeros((bp,) + state_ffn_conv.shape[2:], F32)
        y_p, kp, vp, fp, cp, ffp = _trunk_layer(
            y_p, mod_p, wp, alpha, None, zc, zf, nb=1, tt=prompt_tile, tq=prompt_tq)

        past_len = cache_k.shape[2]
        past = (cache_k[l], cache_v[l], cache_logf[l])
        y_s, ks, vs, fs, cs, ffs = _trunk_layer(
            y_s, mod_s, wp, alpha, past, state_conv[l], state_ffn_conv[l],
            nb=bs, tt=ts, tq=min(past_len, 2048))

        for lst, val in zip(outs, (kp, vp, fp, cp, ffp, ks, vs, fs, cs, ffs)):
            lst.append(val)
    stacked = [jnp.stack(lst) for lst in outs]
    return (y_p, y_s, *stacked)
```

```python
import functools

import jax
import jax.numpy as jnp
from jax import lax
from jax.experimental import pallas as pl
from jax.experimental.pallas import tpu as pltpu

F32 = jnp.float32
BF16 = jnp.bfloat16

LN_EPS = 1e-5
N_MOD = 6
HEAD_DIM = 64
LANES = 128
SUBLANES = 8
HEADS_PER_BLOCK = LANES // HEAD_DIM
ATTN_HEAD_GROUP = 8
CONV_HALO = 32
FFN_HALO = 8
VMEM_LIMIT = 56 * 1024 * 1024
NEG_INIT = -1e30
LOG2E = 1.4426950408889634
N_SPLIT = 3
ONES_ROWS = 16
MIX_GROUP_ROWS = 256


def _sigmoid(x):
    return 0.5 * (jnp.tanh(0.5 * x) + 1.0)


def _log_sigmoid(x):
    return jnp.minimum(x, 0.0) - jnp.log1p(jnp.exp(-jnp.abs(x)))


def _layer_norm(x, g, b):
    mu = jnp.mean(x, axis=-1, keepdims=True)
    xc = x - mu
    var = jnp.mean(xc * xc, axis=-1, keepdims=True)
    return xc * lax.rsqrt(var + LN_EPS) * g + b


def _const_spec(shape):
    zeros = (0,) * len(shape)
    return pl.BlockSpec(shape, lambda *_: zeros)


def _params(*semantics):
    return pltpu.CompilerParams(dimension_semantics=semantics, vmem_limit_bytes=VMEM_LIMIT)


def _ada_kernel(c_ref, w_ref, b_ref, o_ref):
    o_ref[...] = jnp.dot(c_ref[...], w_ref[...], preferred_element_type=F32) + b_ref[...]


def _ada(c_all, w_ada, b_ada):
    rows, d = c_all.shape
    n = w_ada.shape[1]
    return pl.pallas_call(
        _ada_kernel,
        grid=(n // d,),
        in_specs=[pl.BlockSpec((rows, d), lambda j: (0, 0)),
                  pl.BlockSpec((d, d), lambda j: (0, j)),
                  pl.BlockSpec((1, d), lambda j: (0, j))],
        out_specs=pl.BlockSpec((rows, d), lambda j: (0, j)),
        out_shape=jax.ShapeDtypeStruct((rows, n), F32),
        compiler_params=_params("arbitrary"),
        name="ada",
    )(c_all, w_ada, b_ada.reshape(1, n))


def _inproj_kernel(x_ref, sc_ref, sh_ref, wqkv_ref, wf_ref, wglu_ref, wgate_ref, bf_ref,
                   q_ref, k32_ref, v32_ref, kb_ref, vb_ref, logf_ref, glu_ref, gate_ref,
                   *, transposed):
    nb, tt, d = x_ref.shape
    aw = kb_ref.shape[-1]
    cw = glu_ref.shape[-1]
    rows = nb * tt
    u = x_ref[...] * (1.0 + sc_ref[...]) + sh_ref[...]
    ub = u.reshape(rows, d).astype(BF16)

    def proj(w_ref, lo, hi):
        return jnp.dot(ub, w_ref[:, lo:hi], preferred_element_type=F32)

    q = proj(wqkv_ref, 0, aw) * (HEAD_DIM ** -0.5 * LOG2E)
    k = proj(wqkv_ref, aw, 2 * aw)
    kb_ref[...] = k.reshape(nb, tt, aw).astype(BF16)
    v = proj(wqkv_ref, 2 * aw, 3 * aw)
    logf = _log_sigmoid(jnp.dot(ub, wf_ref[...], preferred_element_type=F32) + bf_ref[...])
    if transposed:
        vt = v.T
        q_ref[0] = q.T.astype(BF16)
        k32_ref[0] = k.T
        v32_ref[0] = vt
        vb_ref[0] = vt.astype(BF16)
        logf_ref[0] = logf.T[:logf_ref.shape[1]]
    else:
        q_ref[...] = q.reshape(nb, tt, aw).astype(BF16)
        k32_ref[...] = k.reshape(nb, tt, aw)
        v32_ref[...] = v.reshape(nb, tt, aw)
        vb_ref[...] = v.reshape(nb, tt, aw).astype(BF16)
        nh = logf_ref.shape[-1]
        logf_ref[...] = logf[:, :nh].reshape(nb, tt, nh)

    glu = proj(wglu_ref, 0, cw) * _sigmoid(proj(wglu_ref, cw, 2 * cw))
    glu_ref[...] = glu.reshape(nb, tt, cw)

    gate = _sigmoid(jnp.dot(ub, wgate_ref[...], preferred_element_type=F32))
    gate_ref[...] = gate.reshape(nb, tt, gate.shape[-1]).astype(BF16)


def _inproj(x, sc, sh, wp, nb, tt, transposed):
    b, t, d = x.shape
    aw = wp["w_qkv"].shape[1] // 3
    cw = wp["w_glu"].shape[1] // 2
    gw = wp["w_gate"].shape[1]
    nh = wp["n_heads"]
    assert not transposed or nb == 1
    row = lambda width: pl.BlockSpec((nb, tt, width), lambda i, j: (i, j, 0))
    mod = pl.BlockSpec((nb, 1, d), lambda i, j: (i, 0, 0))
    sds = lambda width, dt: jax.ShapeDtypeStruct((b, t, width), dt)
    if transposed:
        col = lambda width: pl.BlockSpec((1, width, tt), lambda i, j: (i, 0, j))
        sds_t = lambda width, dt: jax.ShapeDtypeStruct((b, width, t), dt)
    else:
        col, sds_t = row, sds
    return pl.pallas_call(
        functools.partial(_inproj_kernel, transposed=transposed),
        grid=(b // nb, t // tt),
        in_specs=[row(d), mod, mod,
                  _const_spec(wp["w_qkv"].shape), _const_spec(wp["w_f"].shape),
                  _const_spec(wp["w_glu"].shape), _const_spec(wp["w_gate"].shape),
                  _const_spec(wp["b_f"].shape)],
        out_specs=[col(aw), col(aw), col(aw), row(aw), col(aw), col(nh), row(cw), row(gw)],
        out_shape=[sds_t(aw, BF16), sds_t(aw, F32), sds_t(aw, F32), sds(aw, BF16), sds_t(aw, BF16),
                   sds_t(nh, F32), sds(cw, F32), sds(gw, BF16)],
        compiler_params=_params("parallel", "parallel"),
        name="inproj",
    )(x, sc, sh, wp["w_qkv"], wp["w_f"], wp["w_glu"], wp["w_gate"], wp["b_f"])


def _split3(x):
    hi = x.astype(BF16)
    r1 = x - hi.astype(F32)
    mid = r1.astype(BF16)
    lo = (r1 - mid.astype(F32)).astype(BF16)
    return hi, mid, lo


def _cumsum_kernel(x_ref, o_ref, piece_ref):
    gb, r, _ = x_ref.shape
    n = gb * r
    li = lax.broadcasted_iota(jnp.int32, (LANES, LANES), 0)
    lj = lax.broadcasted_iota(jnp.int32, (LANES, LANES), 1)
    upper = jnp.where(li <= lj, 1.0, 0.0).astype(BF16)
    ri = lax.broadcasted_iota(jnp.int32, (n, n), 0)
    rj = lax.broadcasted_iota(jnp.int32, (n, n), 1)
    earlier = jnp.where((rj < ri) & (rj // r == ri // r), 1.0, 0.0).astype(BF16)
    x = x_ref[...].reshape(n, LANES)
    within = sum(jnp.dot(p, upper, preferred_element_type=F32) for p in _split3(x))
    total = jnp.broadcast_to(within[:, LANES - 1:LANES], (n, LANES))
    carry = sum(jnp.dot(earlier, p, preferred_element_type=F32) for p in _split3(total))
    f = within + carry
    o_ref[...] = f.reshape(gb, r, LANES)
    for p, piece in enumerate(_split3(f * (-LOG2E))):
        piece_ref[p] = piece.reshape(gb, r, LANES)


def _cumsum_time(logf_bht):
    b, h, l = logf_bht.shape
    tile = 2 * SUBLANES * LANES
    lp = -(-l // tile) * tile
    x = jnp.pad(logf_bht, ((0, 0), (0, 0), (0, lp - l))) if lp != l else logf_bht
    g, r = b * h, lp // LANES
    gb = SUBLANES
    f, pieces = pl.pallas_call(
        _cumsum_kernel,
        grid=(g // gb,),
        in_specs=[pl.BlockSpec((gb, r, LANES), lambda i: (i, 0, 0))],
        out_specs=[pl.BlockSpec((gb, r, LANES), lambda i: (i, 0, 0)),
                   pl.BlockSpec((N_SPLIT, gb, r, LANES), lambda i: (0, i, 0, 0))],
        out_shape=[jax.ShapeDtypeStruct((g, r, LANES), F32),
                   jax.ShapeDtypeStruct((N_SPLIT, g, r, LANES), BF16)],
        compiler_params=_params("parallel"),
        name="cumsum",
    )(x.reshape(g, r, LANES))
    return f.reshape(b, h, lp)[:, :, :l], pieces.reshape(N_SPLIT, b, h, lp)[:, :, :, :l]


def _scores(qa, k):
    return lax.dot_general(qa, k, (((1,), (1,)), ((), ())), preferred_element_type=F32)


def _attn_prompt_kernel(qt_ref, k_ref, kf_ref, vt_ref, o_ref, s_ref, acc_ref, *, tq, tk):
    group = pl.program_id(1)
    i = pl.program_id(2)
    n_heads = qt_ref.shape[1] // HEAD_DIM
    heads = range(n_heads)
    zeros = jnp.zeros((HEAD_DIM, tq), BF16)
    row = lax.broadcasted_iota(jnp.int32, (LANES, tq), 0)
    q_aug = []
    for a in heads:
        first = N_SPLIT * (n_heads * group + a)
        pick = jnp.where((row >= first) & (row < first + N_SPLIT), 1.0, 0.0).astype(BF16)
        pair = a // HEADS_PER_BLOCK
        parts = [qt_ref[0, h * HEAD_DIM:(h + 1) * HEAD_DIM, :] if h == a else zeros
                 for h in range(pair * HEADS_PER_BLOCK, (pair + 1) * HEADS_PER_BLOCK)]
        q_aug.append(jnp.concatenate(parts + [pick], axis=0))

    n_diag = tq // tk

    def produce(slot, blk, diag):
        start = pl.multiple_of(blk * tk, tk)
        kf = kf_ref[0, pl.ds(start, tk), :]
        k_aug = [jnp.concatenate([k_ref[0, pl.ds(start, tk), p * LANES:(p + 1) * LANES], kf], axis=1)
                 for p in range(n_heads // HEADS_PER_BLOCK)]
        hidden = 0 if diag is None else diag * tk
        maxima = []
        for a in heads:
            st = jnp.dot(k_aug[a // HEADS_PER_BLOCK], q_aug[a][:, hidden:],
                         preferred_element_type=F32)
            if diag is not None:
                ki = lax.broadcasted_iota(jnp.int32, st.shape, 0)
                qi = lax.broadcasted_iota(jnp.int32, st.shape, 1)
                st = jnp.where(ki <= qi, st, -jnp.inf)
                if hidden:
                    st = jnp.concatenate([jnp.full((tk, hidden), -jnp.inf, F32), st], axis=1)
            s_ref[slot, a] = st
            maxima.append(jnp.max(st.reshape(tk // SUBLANES, SUBLANES, tq), axis=0))
        return maxima

    ones = jnp.ones((ONES_ROWS, tk), BF16)

    def consume(slot, blk, m_row, m_prev, l):
        start = pl.multiple_of(blk * tk, tk)
        new_l = []
        for a in heads:
            alpha = jnp.exp2(m_prev[a] - m_row[a])
            pt = jnp.exp2(s_ref[slot, a] - m_row[a])
            vt = vt_ref[0, a * HEAD_DIM:(a + 1) * HEAD_DIM, pl.ds(start, tk)]
            pv = jnp.dot(jnp.concatenate([vt, ones], axis=0), pt.astype(BF16),
                         preferred_element_type=F32)
            acc_ref[a] = alpha * acc_ref[a] + pv[:HEAD_DIM]
            new_l.append(alpha * l[a] + pv[HEAD_DIM:HEAD_DIM + 1])
        return new_l

    def step(state, cons, prod):
        m8, m_prev, l = state
        m_row = [jnp.max(m8[a], axis=0, keepdims=True) for a in heads]
        if prod is not None:
            m8 = [jnp.maximum(m8[a], x) for a, x in zip(heads, produce(*prod))]
        l = consume(*cons, m_row, m_prev, l)
        return m8, m_row, l

    for a in heads:
        acc_ref[a] = jnp.zeros((HEAD_DIM, tq), F32)
    first_diag = n_diag * i
    m8 = produce(0, first_diag, 0)
    state = (m8, [jnp.full((1, tq), NEG_INIT, F32) for _ in heads],
             [jnp.zeros((1, tq), F32) for _ in heads])
    for dg in range(1, n_diag):
        state = step(state, ((dg - 1) % 2, first_diag + dg - 1), (dg % 2, first_diag + dg, dg))
    pending = (n_diag - 1) % 2
    assert pending == 1 and n_diag == 2

    def pair(t, carry):
        blk = 2 * t
        prev_blk = jnp.where(t == 0, first_diag + n_diag - 1, blk - 1)
        state = step(tuple(list(x) for x in carry), (1, prev_blk), (0, blk, None))
        state = step(state, (0, blk), (1, blk + 1, None))
        return tuple(tuple(x) for x in state)

    n_pairs = i * n_diag // 2
    state = lax.fori_loop(0, n_pairs // 2, lambda u, c: pair(2 * u + 1, pair(2 * u, c)),
                          tuple(tuple(x) for x in state))
    state = lax.cond(n_pairs % 2 == 1, lambda c: pair(n_pairs - 1, c), lambda c: c, state)
    last_blk = jnp.where(i == 0, first_diag + n_diag - 1, first_diag - 1)
    _, _, l = step(tuple(list(x) for x in state), (1, last_blk), None)
    out_t = jnp.concatenate([acc_ref[a] / l[a] for a in heads], axis=0)
    o_ref[0] = out_t.T.astype(o_ref.dtype)


def _attn_prompt(qt, kb, kf, vt, tq):
    b, aw, t = qt.shape
    tk = tq // 2
    gw = ATTN_HEAD_GROUP * HEAD_DIM
    return pl.pallas_call(
        functools.partial(_attn_prompt_kernel, tq=tq, tk=tk),
        scratch_shapes=[pltpu.VMEM((2, ATTN_HEAD_GROUP, tk, tq), F32),
                        pltpu.VMEM((ATTN_HEAD_GROUP, HEAD_DIM, tq), F32)],
        grid=(b, aw // gw, t // tq),
        in_specs=[pl.BlockSpec((1, gw, tq), lambda bi, g, i: (bi, g, i)),
                  pl.BlockSpec((1, t, gw), lambda bi, g, i: (bi, 0, g)),
                  pl.BlockSpec((1, t, LANES), lambda bi, g, i: (bi, 0, 0)),
                  pl.BlockSpec((1, gw, t), lambda bi, g, i: (bi, g, 0))],
        out_specs=pl.BlockSpec((1, tq, gw), lambda bi, g, i: (bi, i, g)),
        out_shape=jax.ShapeDtypeStruct((b, t, aw), BF16),
        compiler_params=_params("parallel", "parallel", "parallel"),
        name="attn_prompt",
    )(qt, kb, kf, vt)


def _attn_cached_kernel(q_ref, ckt_ref, cvt_ref, kn_ref, vn_ref, fp_ref, fn_ref, o_ref,
                        m_ref, l_ref, acc_ref):
    j = pl.program_id(1)
    n_chunks = pl.num_programs(1) - 1
    n_heads = ckt_ref.shape[1]

    @pl.when(j == 0)
    def _():
        m_ref[...] = jnp.full(m_ref.shape, NEG_INIT, F32)
        l_ref[...] = jnp.zeros(l_ref.shape, F32)
        acc_ref[...] = jnp.zeros(acc_ref.shape, F32)

    def update(h, s, pv_of):
        m_old = m_ref[h]
        m_new = jnp.maximum(m_old, jnp.max(s, axis=-1, keepdims=True))
        alpha = jnp.exp2(m_old - m_new)
        p = jnp.exp2(s - m_new)
        l_ref[h] = alpha * l_ref[h] + jnp.sum(p, axis=-1, keepdims=True)
        acc_ref[h] = alpha * acc_ref[h] + pv_of(p.astype(BF16))
        m_ref[h] = m_new

    def q_head(h):
        return q_ref[0, :, h * HEAD_DIM:(h + 1) * HEAD_DIM]

    @pl.when(j < n_chunks)
    def _():
        for h in range(n_heads):
            s = jnp.dot(q_head(h), ckt_ref[0, h].astype(BF16), preferred_element_type=F32)
            s = s - LOG2E * fp_ref[0, h:h + 1, :]
            update(h, s, lambda p, h=h: _scores(p, cvt_ref[0, h].astype(BF16)))

    @pl.when(j == n_chunks)
    def _():
        outs = []
        for h in range(n_heads):
            lanes = slice(h * HEAD_DIM, (h + 1) * HEAD_DIM)
            s = _scores(q_head(h), kn_ref[0, :, lanes]) - LOG2E * fn_ref[0, h:h + 1, :]
            qi = lax.broadcasted_iota(jnp.int32, s.shape, 0)
            ki = lax.broadcasted_iota(jnp.int32, s.shape, 1)
            s = jnp.where(ki <= qi, s, -jnp.inf)
            update(h, s, lambda p, lanes=lanes: jnp.dot(p, vn_ref[0, :, lanes],
                                                        preferred_element_type=F32))
            outs.append(acc_ref[h] / l_ref[h])
        o_ref[0] = jnp.concatenate(outs, axis=1).astype(o_ref.dtype)


def _attn_cached(q, cache_kt, cache_vt, kb, vb, f_past, f_new, chunk):
    b, t, aw = q.shape
    _, nh, dh, p = cache_kt.shape
    n_chunks = p // chunk
    last = n_chunks - 1
    new = pl.BlockSpec((1, t, aw), lambda bi, j: (bi, 0, 0))
    past = pl.BlockSpec((1, nh, dh, chunk), lambda bi, j: (bi, 0, 0, jnp.minimum(j, last)))
    return pl.pallas_call(
        _attn_cached_kernel,
        grid=(b, n_chunks + 1),
        in_specs=[new, past, past, new, new,
                  pl.BlockSpec((1, nh, chunk), lambda bi, j: (bi, 0, jnp.minimum(j, last))),
                  pl.BlockSpec((1, nh, t), lambda bi, j: (bi, 0, 0))],
        out_specs=new,
        out_shape=jax.ShapeDtypeStruct((b, t, aw), BF16),
        scratch_shapes=[pltpu.VMEM((nh, t, 1), F32), pltpu.VMEM((nh, t, 1), F32),
                        pltpu.VMEM((nh, t, dh), F32)],
        compiler_params=_params("parallel", "arbitrary"),
        name="attn_cached",
    )(q, cache_kt, cache_vt, kb, vb, f_past, f_new)


def _mix_kernel(glu_ref, hist_ref, attn_ref, gate_ref, x_ref, g1_ref,
                cw_ref, cb_ref, clg_ref, clb_ref, wap_ref, wcp_ref, wo_ref, l1g_ref, l1b_ref,
                x1_ref, sh_ref, *, alpha, row_chunk, group_rows):
    nb, tt, cw = glu_ref.shape
    d = x_ref.shape[-1]
    taps = cw_ref.shape[0]
    n_hist = hist_ref.shape[1]

    ext_ref = sh_ref.at[0]
    length = CONV_HALO + tt

    @pl.when(pl.program_id(1) == 0)
    def _():
        ext_ref[:, 0:SUBLANES, :] = jnp.zeros((nb, SUBLANES, cw), F32)
        ext_ref[:, CONV_HALO - n_hist:CONV_HALO, :] = hist_ref[...]

    ext_ref[:, CONV_HALO:length, :] = glu_ref[...]
    for r in range(1, SUBLANES):
        sh_ref[r, :, 0:length - SUBLANES, :] = ext_ref[:, r:r + length - SUBLANES, :]

    first = CONV_HALO - (taps - 1)
    def conv_rows(b, r0):
        acc = jnp.zeros((row_chunk // SUBLANES, SUBLANES, cw), F32)
        for k in range(taps):
            shift = (first + k) % SUBLANES
            base = r0 + first + k - shift
            rows_k = sh_ref[shift, b, base:base + row_chunk, :]
            acc = acc + cw_ref[k] * rows_k.reshape(row_chunk // SUBLANES, SUBLANES, cw)
        h = _layer_norm(acc.reshape(row_chunk, cw) + cb_ref[...], clg_ref[...], clb_ref[...])
        return (h * _sigmoid(h)).astype(BF16)

    for g in range(tt // group_rows):
        rs = slice(g * group_rows, (g + 1) * group_rows)
        n = nb * group_rows
        hc = jnp.concatenate([conv_rows(b, rs.start + c * row_chunk)
                              for b in range(nb) for c in range(group_rows // row_chunk)], axis=0)
        ya = jnp.dot(attn_ref[:, rs, :].reshape(n, attn_ref.shape[-1]), wap_ref[...],
                     preferred_element_type=F32)
        yb = jnp.dot(hc, wcp_ref[...], preferred_element_type=F32)
        gate = gate_ref[:, rs, :].reshape(n, 2 * d)
        merged = gate[:, :d] * ya.astype(BF16) + gate[:, d:] * yb.astype(BF16)
        z = jnp.dot(merged, wo_ref[...], preferred_element_type=F32)
        r = alpha * x_ref[:, rs, :] + g1_ref[...] * z.reshape(nb, group_rows, d)
        x1_ref[:, rs, :] = _layer_norm(r, l1g_ref[...], l1b_ref[...])

    ext_ref[:, 0:CONV_HALO, :] = ext_ref[:, tt:tt + CONV_HALO, :]


def _mix(glu, hist, attn, gate, x, g1, wp, alpha, nb, tt):
    b, t, d = x.shape
    cw = glu.shape[-1]
    row_chunk = min(tt, 64)
    group_rows = min(tt, MIX_GROUP_ROWS) if nb == 1 else tt
    row = lambda width: pl.BlockSpec((nb, tt, width), lambda i, j: (i, j, 0))
    per_seq = lambda arr: pl.BlockSpec((nb,) + arr.shape[1:], lambda i, j: (i, 0, 0))
    consts = [wp["conv_w"], wp["conv_b"], wp["conv_ln_g"], wp["conv_ln_b"], wp["w_attn_proj"],
              wp["w_conv_proj"], wp["w_out"], wp["ln1_g"], wp["ln1_b"]]
    return pl.pallas_call(
        functools.partial(_mix_kernel, alpha=alpha, row_chunk=row_chunk, group_rows=group_rows),
        grid=(b // nb, t // tt),
        in_specs=[row(cw), per_seq(hist), row(attn.shape[-1]), row(gate.shape[-1]), row(d),
                  per_seq(g1)] + [_const_spec(c.shape) for c in consts],
        out_specs=row(d),
        out_shape=jax.ShapeDtypeStruct((b, t, d), F32),
        scratch_shapes=[pltpu.VMEM((SUBLANES, nb, CONV_HALO + tt, cw), F32)],
        compiler_params=_params("parallel", "arbitrary"),
        name="mix",
    )(glu, hist, attn, gate, x, g1, *consts)


def _ffn_kernel(x1_ref, sc_ref, sh_ref, g2_ref, hist_ref, wua_ref, wuv_ref, fw_ref, fb_ref,
                wd_ref, l2g_ref, l2b_ref, y_ref, state_ref, ext_ref, *, alpha):
    nb, tt, d = x1_ref.shape
    dff = wua_ref.shape[1]
    taps = fw_ref.shape[0]
    n_hist = hist_ref.shape[1]
    rows = nb * tt

    @pl.when(pl.program_id(1) == 0)
    def _():
        ext_ref[:, FFN_HALO - n_hist:FFN_HALO, :] = hist_ref[...]

    x1 = x1_ref[...]
    u = x1 * (1.0 + sc_ref[...]) + sh_ref[...]
    ub = u.reshape(rows, d).astype(BF16)
    a2 = jnp.dot(ub, wua_ref[...], preferred_element_type=F32)
    ext_ref[:, FFN_HALO:FFN_HALO + tt, :] = a2.reshape(nb, tt, dff)
    state_ref[...] = ext_ref[:, FFN_HALO + tt - n_hist:FFN_HALO + tt, :]

    first = FFN_HALO - (taps - 1)
    conv = fb_ref[...].reshape(1, 1, dff)
    for k in range(taps):
        conv = conv + fw_ref[k:k + 1, :].reshape(1, 1, dff) * ext_ref[:, first + k:first + k + tt, :]
    v2 = jnp.dot(ub, wuv_ref[...], preferred_element_type=F32)
    conv = conv.reshape(rows, dff)
    h = conv * _sigmoid(conv) * v2

    ext_ref[:, 0:FFN_HALO, :] = ext_ref[:, tt:tt + FFN_HALO, :]

    z = jnp.dot(h.astype(BF16), wd_ref[...], preferred_element_type=F32).reshape(nb, tt, d)
    r = alpha * x1 + g2_ref[...] * z
    y_ref[...] = _layer_norm(r, l2g_ref[...], l2b_ref[...])


def _ffn(x1, sc, sh, g2, hist, wp, alpha, nb, tt):
    b, t, d = x1.shape
    dff = wp["w_up_a"].shape[1]
    n_hist = hist.shape[1]
    row = pl.BlockSpec((nb, tt, d), lambda i, j: (i, j, 0))
    per_seq = lambda arr: pl.BlockSpec((nb,) + arr.shape[1:], lambda i, j: (i, 0, 0))
    consts = [wp["w_up_a"], wp["w_up_v"], wp["ffn_conv_w"], wp["ffn_conv_b"], wp["w_down"],
              wp["ln2_g"], wp["ln2_b"]]
    return pl.pallas_call(
        functools.partial(_ffn_kernel, alpha=alpha),
        grid=(b // nb, t // tt),
        in_specs=[row, per_seq(sc), per_seq(sh), per_seq(g2), per_seq(hist)]
                 + [_const_spec(c.shape) for c in consts],
        out_specs=[row, pl.BlockSpec((nb, n_hist, dff), lambda i, j: (i, 0, 0))],
        out_shape=[jax.ShapeDtypeStruct((b, t, d), F32),
                   jax.ShapeDtypeStruct((b, n_hist, dff), F32)],
        scratch_shapes=[pltpu.VMEM((nb, FFN_HALO + tt, dff), F32)],
        compiler_params=_params("parallel", "arbitrary"),
        name="ffn",
    )(x1, sc, sh, g2, hist, *consts)


def _prepare_weights(w_in, b_f, conv_w, conv_b, conv_ln_g, conv_ln_b, w_attn_proj, w_conv_proj,
                     w_out, ln1_g, ln1_b, w_up, ffn_conv_w, ffn_conv_b, w_down, ln2_g, ln2_b):
    d = w_in.shape[0]
    aw = w_attn_proj.shape[0]
    cw = w_conv_proj.shape[0]
    nh = b_f.shape[0]
    dff = w_down.shape[0]
    s_v, s_f, s_glu = 3 * aw, 3 * aw + nh, 3 * aw + nh + 2 * cw
    row = lambda v: v.reshape(1, -1)
    return dict(
        w_qkv=w_in[:, :s_v].astype(BF16),
        w_f=jnp.pad(w_in[:, s_v:s_f], ((0, 0), (0, LANES - nh))).astype(BF16),
        w_glu=w_in[:, s_f:s_glu].astype(BF16),
        w_gate=w_in[:, s_glu:].astype(BF16),
        b_f=jnp.pad(row(b_f), ((0, 0), (0, LANES - nh))), n_heads=nh,
        conv_w=jnp.broadcast_to(conv_w[:, None, :], (conv_w.shape[0], SUBLANES, cw)),
        conv_b=row(conv_b), conv_ln_g=row(conv_ln_g), conv_ln_b=row(conv_ln_b),
        w_attn_proj=w_attn_proj.astype(BF16), w_conv_proj=w_conv_proj.astype(BF16),
        w_out=w_out.astype(BF16), ln1_g=row(ln1_g), ln1_b=row(ln1_b),
        w_up_a=w_up[:, :dff].astype(BF16), w_up_v=w_up[:, dff:].astype(BF16),
        ffn_conv_w=ffn_conv_w, ffn_conv_b=row(ffn_conv_b),
        w_down=w_down.astype(BF16), ln2_g=row(ln2_g), ln2_b=row(ln2_b),
    )


def _trunk_layer(x, mod, wp, alpha, past, conv_hist, ffn_hist, nb, tt, tq):
    b, t, d = x.shape
    sh1, sc1, g1, sh2, sc2, g2 = mod
    nh = wp["n_heads"]
    q, k32, v32, kb, vb, logf, glu, gate = _inproj(x, sc1, sh1, wp, nb, tt, transposed=past is None)

    if past is None:
        _, pieces = _cumsum_time(logf)
        kf = jnp.transpose(pieces, (1, 3, 2, 0)).reshape(b, t, -1)
        kf = jnp.pad(kf, ((0, 0), (0, 0), (0, LANES - kf.shape[-1])))
        attn = _attn_prompt(q, kb, kf, vb, tq)
        by_row = lambda a: jnp.transpose(a.reshape(b, nh, -1, t), (0, 3, 1, 2))
        k_out, v_out, logf_out = by_row(k32), by_row(v32), jnp.transpose(logf, (0, 2, 1))
    else:
        cache_k, cache_v, cache_logf = past
        p = cache_k.shape[1]
        f_all, _ = _cumsum_time(jnp.concatenate(
            [jnp.transpose(cache_logf, (0, 2, 1)), jnp.transpose(logf, (0, 2, 1))], axis=2))
        by_feature = lambda a: jnp.transpose(a, (0, 2, 3, 1))
        attn = _attn_cached(q, by_feature(cache_k), by_feature(cache_v), kb, vb,
                            f_all[:, :, :p], f_all[:, :, p:], tq)
        heads = lambda a: a.reshape(b, t, nh, -1)
        k_out, v_out, logf_out = heads(k32), heads(v32), logf

    x1 = _mix(glu, conv_hist, attn, gate, x, g1, wp, alpha, nb, tt)
    y, ffn_state = _ffn(x1, sc2, sh2, g2, ffn_hist, wp, alpha, nb, tt)
    conv_state = glu[:, t - conv_hist.shape[1]:, :]
    return y, k_out, v_out, logf_out, conv_state, ffn_state


def kernel(x_prompt, x_sample, c_prompt, c_sample, cache_k, cache_v, cache_logf, state_conv, state_ffn_conv, w_ada, b_ada, w_in, b_f, conv_w, conv_b, conv_ln_g, conv_ln_b, w_attn_proj, w_conv_proj, w_out, ln1_g, ln1_b, w_up, ffn_conv_w, ffn_conv_b, w_down, ln2_g, ln2_b):
    depth = w_ada.shape[0]
    bp, tp, d = x_prompt.shape
    bs, ts, _ = x_sample.shape
    nh = b_f.shape[1]
    alpha = float((2 * depth) ** 0.25)
    prompt_tile = min(tp, 512)
    prompt_tq = min(tp, 512)

    y_p, y_s = x_prompt, x_sample
    c_all = jnp.concatenate([c_prompt, c_sample], axis=0)
    outs = [[] for _ in range(10)]
    for l in range(depth):
        wp = _prepare_weights(w_in[l], b_f[l], conv_w[l], conv_b[l], conv_ln_g[l], conv_ln_b[l],
                              w_attn_proj[l], w_conv_proj[l], w_out[l], ln1_g[l], ln1_b[l],
                              w_up[l], ffn_conv_w[l], ffn_conv_b[l], w_down[l], ln2_g[l], ln2_b[l])
        mod = _ada(c_all, w_ada[l], b_ada[l])
        mod_p = [mod[:bp, i * d:(i + 1) * d].reshape(bp, 1, d) for i in range(N_MOD)]
        mod_s = [mod[bp:, i * d:(i + 1) * d].reshape(bs, 1, d) for i in range(N_MOD)]

        zc = jnp.zeros((bp,) + state_conv.shape[2:], F32)
        zf = jnp.zeros((bp,) + state_ffn_conv.shape[2:], F32)
        y_p, kp, vp, fp, cp, ffp = _trunk_layer(
            y_p, mod_p, wp, alpha, None, zc, zf, nb=1, tt=prompt_tile, tq=prompt_tq)

        past_len = cache_k.shape[2]
        past = (cache_k[l], cache_v[l], cache_logf[l])
        y_s, ks, vs, fs, cs, ffs = _trunk_layer(
            y_s, mod_s, wp, alpha, past, state_conv[l], state_ffn_conv[l],
            nb=bs, tt=ts, tq=min(past_len, 4096))

        for lst, val in zip(outs, (kp, vp, fp, cp, ffp, ks, vs, fs, cs, ffs)):
            lst.append(val)
    stacked = [jnp.stack(lst) for lst in outs]
    return (y_p, y_s, *stacked)
```
